```python
import math
import jax, jax.numpy as jnp
from jax import lax
import numpy as np

D_MODEL = 2048
BATCH = 4
SEQ = 4096
DEPTH = 1

DN_HEADS = 16
DN_DK = 128
DN_DV = 128
DN_CONV = 4
DN_CHUNK = 64
DN_QK = DN_HEADS * DN_DK
DN_VW = DN_HEADS * DN_DV
MB_HEADS = 16
MB_HD = 128
MB_W = MB_HEADS * MB_HD
MB_BLOCK = 256
MB_TOPK = 3
MB_Q_CHUNK = 32
RP_BUCKETS = 32
RP_MAX_DIST = 1024
D_FF = 5632
NORM_EPS = 1e-6
N_ADA = 9
IN_SIZES = (2 * DN_QK + DN_VW, DN_VW, DN_HEADS, DN_HEADS, MB_W, MB_W, MB_W, D_MODEL, D_MODEL)
IN_WIDTH = 2 * DN_QK + 2 * DN_VW + 2 * DN_HEADS + 3 * MB_W + 2 * D_MODEL

kernel_name = "hybrid_deltanet_moba_macaron_adaln"


def rms_norm(x, gain):
    xf = x.astype(jnp.float32)
    y = xf * lax.rsqrt(jnp.mean(xf * xf, axis=-1, keepdims=True) + NORM_EPS)
    return (y * gain.astype(jnp.float32)).astype(x.dtype)


def l2norm(x):
    xf = x.astype(jnp.float32)
    return xf * lax.rsqrt(jnp.sum(xf * xf, axis=-1, keepdims=True) + NORM_EPS)


def modulate(xn, shift, scale):
    return xn * (1.0 + scale) + shift


def swiglu(x, w1, w3, w2):
    return (jax.nn.silu(x @ w1) * (x @ w3)) @ w2


def causal_depthwise_conv(x, w):
    K = w.shape[0]
    T = x.shape[1]
    xp = jnp.pad(x, ((0, 0), (K - 1, 0), (0, 0)))
    return sum(xp[:, j:j + T, :] * w[j] for j in range(K))


def gated_delta_chunked(q, k, v, g, beta):
    B, T, H, DK = q.shape
    DV = v.shape[-1]
    C = DN_CHUNK
    N = T // C
    f32 = jnp.float32

    def to_chunks(a):
        a = a.astype(f32).reshape((B, N, C, H) + a.shape[3:])
        return jnp.moveaxis(a, 3, 1)

    q, k, v, g, beta = (to_chunks(a) for a in (q, k, v, g, beta))
    q = q * (DK ** -0.5)
    gc = jnp.cumsum(g, axis=-1)
    idx = jnp.arange(C)
    incl = idx[:, None] >= idx[None, :]
    strict = idx[:, None] > idx[None, :]
    decay = jnp.exp(jnp.where(incl, gc[..., :, None] - gc[..., None, :], -jnp.inf))
    k_beta = k * beta[..., None]
    Lmat = jnp.where(strict, jnp.einsum('bhncd,bhnsd->bhncs', k_beta, k) * decay, 0.0)
    eye = jnp.eye(C, dtype=f32)
    Tm = lax.linalg.triangular_solve(eye + Lmat, jnp.broadcast_to(eye, Lmat.shape),
                                     left_side=True, lower=True, unit_diagonal=True)
    u = jnp.einsum('bhncs,bhnsd->bhncd', Tm, v * beta[..., None])
    w = jnp.einsum('bhncs,bhnsd->bhncd', Tm, k_beta * jnp.exp(gc)[..., None])
    qk = jnp.einsum('bhncd,bhnsd->bhncs', q, k) * decay
    q_dec = q * jnp.exp(gc)[..., None]
    k_dec = k * jnp.exp(gc[..., -1:] - gc)[..., None]
    g_tot = jnp.exp(gc[..., -1])

    def step(S, xs):
        qk_c, qd_c, kd_c, u_c, w_c, gt_c = xs
        v_new = u_c - jnp.einsum('bhcd,bhde->bhce', w_c, S)
        o = jnp.einsum('bhcd,bhde->bhce', qd_c, S) + jnp.einsum('bhcs,bhse->bhce', qk_c, v_new)
        S = S * gt_c[..., None, None] + jnp.einsum('bhcd,bhce->bhde', kd_c, v_new)
        return S, o

    xs = tuple(jnp.moveaxis(a, 2, 0) for a in (qk, q_dec, k_dec, u, w, g_tot))
    S0 = jnp.zeros((B, H, DK, DV), f32)
    _, o = lax.scan(step, S0, xs)
    return jnp.transpose(o, (1, 0, 3, 2, 4)).reshape(B, T, H, DV)


def t5_bucket(dist):
    max_exact = RP_BUCKETS // 2
    d = jnp.maximum(dist, 0)
    large = max_exact + (jnp.log(jnp.maximum(d, 1).astype(jnp.float32) / max_exact)
                         / math.log(RP_MAX_DIST / max_exact) * (RP_BUCKETS - max_exact)).astype(jnp.int32)
    large = jnp.minimum(large, RP_BUCKETS - 1)
    return jnp.where(d < max_exact, d, large)


def moba_attention(q, k, v, rel_bias):
    B, T, H, D = q.shape
    blk = MB_BLOCK
    nb = -(-T // blk)
    Tp = nb * blk
    f32 = jnp.float32
    q = jnp.transpose(q, (0, 2, 1, 3)).astype(f32) * (D ** -0.5)
    pad = ((0, 0), (0, 0), (0, Tp - T), (0, 0))
    kb = jnp.pad(jnp.transpose(k, (0, 2, 1, 3)).astype(f32), pad).reshape(B, H, nb, blk, D)
    vb = jnp.pad(jnp.transpose(v, (0, 2, 1, 3)).astype(f32), pad).reshape(B, H, nb, blk, D)
    k_mean = jnp.mean(kb, axis=3)
    pos = jnp.arange(T, dtype=jnp.int32)
    qblk = pos // blk
    gate = jnp.einsum('bhtd,bhjd->bhtj', q, k_mean)
    past = jnp.arange(nb)[None, :] < qblk[:, None]
    gate = jnp.where(past, gate, -jnp.inf)
    n_sel = min(MB_TOPK, nb)
    _, sel = lax.top_k(gate, n_sel)
    sel_valid = sel < qblk[:, None]
    rb = rel_bias.T.astype(f32)
    QC = MB_Q_CHUNK
    nq = T // QC

    def chunks(a):
        a = a.reshape(a.shape[:2] + (nq, QC) + a.shape[3:])
        return jnp.moveaxis(a, 2, 0)

    xs = (chunks(q), chunks(sel), chunks(sel_valid), pos.reshape(nq, QC))
    offs = jnp.arange(blk, dtype=jnp.int32)
    bi = jnp.arange(B)[:, None, None, None]
    hi = jnp.arange(H)[None, :, None, None]
    hi5 = jnp.arange(H)[None, :, None, None, None]

    def attend(args):
        qc, sc, vc, qp = args
        own = qp[0] // blk
        k_own = lax.dynamic_index_in_dim(kb, own, axis=2, keepdims=False)
        v_own = lax.dynamic_index_in_dim(vb, own, axis=2, keepdims=False)
        k_sel = kb[bi, hi, sc]
        v_sel = vb[bi, hi, sc]
        s_sel = jnp.einsum('bhqd,bhqskd->bhqsk', qc, k_sel)
        dist_sel = qp[None, None, :, None, None] - (sc[..., None] * blk + offs)
        s_sel = jnp.where(vc[..., None], s_sel + rb[hi5, t5_bucket(dist_sel)], -jnp.inf)
        dist_own = qp[:, None] - (own * blk + offs)[None, :]
        s_own = jnp.einsum('bhqd,bhkd->bhqk', qc, k_own) + rb[:, t5_bucket(dist_own)]
        s_own = jnp.where(dist_own >= 0, s_own, -jnp.inf)
        ns = sc.shape[-1] * blk
        logits = jnp.concatenate([s_sel.reshape(B, H, QC, ns), s_own], axis=-1)
        p = jax.nn.softmax(logits, axis=-1)
        p_sel = p[..., :ns].reshape(B, H, QC, sc.shape[-1], blk)
        p_own = p[..., ns:]
        return (jnp.einsum('bhqsk,bhqskd->bhqd', p_sel, v_sel)
                + jnp.einsum('bhqk,bhkd->bhqd', p_own, v_own))

    o = lax.map(attend, xs)
    o = jnp.transpose(o, (1, 0, 3, 2, 4)).reshape(B, T, H * D)
    return o


def hybrid_mixer(u, w_in, conv_w, a_log, dt_bias, dn_norm_g, q_norm_g, k_norm_g, rel_bias,
                 w_proj_a, w_proj_b, w_out):
    B, T, _ = u.shape
    f32 = jnp.float32
    proj = u @ w_in
    cuts, acc = [], 0
    for s in IN_SIZES[:-1]:
        acc += s
        cuts.append(acc)
    dn_qkv, dn_z, dn_b, dn_a, mb_q, mb_k, mb_v, gate_a, gate_b = jnp.split(proj, cuts, axis=-1)
    dn_qkv = jax.nn.silu(causal_depthwise_conv(dn_qkv, conv_w))
    q, k, v = jnp.split(dn_qkv, [DN_QK, 2 * DN_QK], axis=-1)
    q = l2norm(q.reshape(B, T, DN_HEADS, DN_DK))
    k = l2norm(k.reshape(B, T, DN_HEADS, DN_DK))
    v = v.reshape(B, T, DN_HEADS, DN_DV)
    beta = jax.nn.sigmoid(dn_b.astype(f32))
    g = -jnp.exp(a_log.astype(f32)) * jax.nn.softplus(dn_a.astype(f32) + dt_bias.astype(f32))
    o = gated_delta_chunked(q, k, v, g, beta)
    o = rms_norm(o, dn_norm_g) * jax.nn.silu(dn_z.reshape(B, T, DN_HEADS, DN_DV).astype(f32))
    y_a = o.reshape(B, T, DN_VW).astype(u.dtype)
    qm = rms_norm(mb_q.reshape(B, T, MB_HEADS, MB_HD), q_norm_g)
    km = rms_norm(mb_k.reshape(B, T, MB_HEADS, MB_HD), k_norm_g)
    vm = mb_v.reshape(B, T, MB_HEADS, MB_HD)
    y_b = moba_attention(qm, km, vm, rel_bias).astype(u.dtype)
    merged = jax.nn.sigmoid(gate_a) * (y_a @ w_proj_a) + jax.nn.sigmoid(gate_b) * (y_b @ w_proj_b)
    return merged @ w_out


def setup_inputs(seed: int = 0) -> dict:
    key = jax.random.key(seed)
    ks = jax.random.split(key, 26)
    f32 = jnp.float32
    L, D = DEPTH, D_MODEL
    conv_ch = 2 * DN_QK + DN_VW

    def nrm(k, shape, scale):
        return jax.random.normal(k, shape, f32) * scale

    def gain(k, shape):
        return 1.0 + 0.02 * jax.random.normal(k, shape, f32)

    dt = jnp.exp(jax.random.uniform(ks[12], (L, DN_HEADS), f32, math.log(1e-3), math.log(1e-1)))
    return {
        "x": nrm(ks[0], (BATCH, SEQ, D), 1.0),
        "c": nrm(ks[1], (BATCH, D), 1.0),
        "ada_w": nrm(ks[2], (L, D, N_ADA * D), 0.5 * D ** -0.5),
        "ada_b": nrm(ks[3], (L, N_ADA * D), 0.01),
        "norm1_g": gain(ks[4], (L, D)),
        "ffn1_w1": nrm(ks[5], (L, D, D_FF), D ** -0.5),
        "ffn1_w3": nrm(ks[6], (L, D, D_FF), D ** -0.5),
        "ffn1_w2": nrm(ks[7], (L, D_FF, D), D_FF ** -0.5),
        "norm2_g": gain(ks[8], (L, D)),
        "w_in": nrm(ks[9], (L, D, IN_WIDTH), D ** -0.5),
        "dn_conv_w": nrm(ks[10], (L, DN_CONV, conv_ch), 0.5),
        "dn_a_log": jnp.log(jax.random.uniform(ks[11], (L, DN_HEADS), f32, 1.0, 16.0)),
        "dn_dt_bias": dt + jnp.log(-jnp.expm1(-dt)),
        "dn_norm_g": gain(ks[13], (L, DN_DV)),
        "mb_q_norm_g": gain(ks[14], (L, MB_HD)),
        "mb_k_norm_g": gain(ks[15], (L, MB_HD)),
        "rel_bias": nrm(ks[16], (RP_BUCKETS, MB_HEADS), 0.5),
        "w_proj_a": nrm(ks[17], (L, DN_VW, D), DN_VW ** -0.5),
        "w_proj_b": nrm(ks[18], (L, MB_W, D), MB_W ** -0.5),
        "w_out": nrm(ks[19], (L, D, D), D ** -0.5),
        "norm3_g": gain(ks[20], (L, D)),
        "ffn2_w1": nrm(ks[21], (L, D, D_FF), D ** -0.5),
        "ffn2_w3": nrm(ks[22], (L, D, D_FF), D ** -0.5),
        "ffn2_w2": nrm(ks[23], (L, D_FF, D), D_FF ** -0.5),
    }


def reference(x, c, ada_w, ada_b, norm1_g, ffn1_w1, ffn1_w3, ffn1_w2, norm2_g, w_in, dn_conv_w,
              dn_a_log, dn_dt_bias, dn_norm_g, mb_q_norm_g, mb_k_norm_g, rel_bias, w_proj_a,
              w_proj_b, w_out, norm3_g, ffn2_w1, ffn2_w3, ffn2_w2):
    h = x
    for l in range(DEPTH):
        ada = jax.nn.silu(c) @ ada_w[l] + ada_b[l]
        sh1, sc1, g1, sh2, sc2, g2, sh3, sc3, g3 = jnp.split(ada[:, None, :], N_ADA, axis=-1)
        u = modulate(rms_norm(h, norm1_g[l]), sh1, sc1)
        h = h + 0.5 * g1 * swiglu(u, ffn1_w1[l], ffn1_w3[l], ffn1_w2[l])
        u = modulate(rms_norm(h, norm2_g[l]), sh2, sc2)
        h = h + g2 * hybrid_mixer(u, w_in[l], dn_conv_w[l], dn_a_log[l], dn_dt_bias[l], dn_norm_g[l],
                                  mb_q_norm_g[l], mb_k_norm_g[l], rel_bias, w_proj_a[l], w_proj_b[l],
                                  w_out[l])
        u = modulate(rms_norm(h, norm3_g[l]), sh3, sc3)
        h = h + 0.5 * g3 * swiglu(u, ffn2_w1[l], ffn2_w3[l], ffn2_w2[l])
    return h
```

```python
import functools
import math

import jax
import jax.numpy as jnp
from jax import lax
from jax.experimental import pallas as pl
from jax.experimental.pallas import tpu as pltpu

F32 = jnp.float32
BF16 = jnp.bfloat16

LANES = 128
NORM_EPS = 1e-6
N_ADA = 9
HEADS = 16
HEAD_DIM = 128
DN_CONV = 4
DN_CHUNK = 128
DN_HEAD_BLOCK = 4
MB_BLOCK = 256
MB_TOPK = 3
RP_BUCKETS = 32
RP_MAX_DIST = 1024
NEG_BIG = -1e30
VMEM_LIMIT = 56 * 1024 * 1024

NT_DIMS = (((1,), (1,)), ((), ()))
NN_DIMS = (((1,), (0,)), ((), ()))
TN_DIMS = (((0,), (0,)), ((), ()))


def _dot(a, b, dims=NN_DIMS):
    return lax.dot_general(a.astype(BF16), b.astype(BF16), dims, preferred_element_type=F32)


def _split_bf16(x):
    hi = x.astype(BF16)
    lo = (x - hi.astype(F32)).astype(BF16)
    return hi, lo


def _dot3(a, b, dims=NN_DIMS):
    ah, al = _split_bf16(a)
    bh, bl = _split_bf16(b)
    f = lambda x, y: lax.dot_general(x, y, dims, preferred_element_type=F32)
    return f(ah, bh) + (f(ah, bl) + f(al, bh))


def _silu(x):
    return x * jax.nn.sigmoid(x)


def _params(*sem):
    return pltpu.CompilerParams(dimension_semantics=sem, vmem_limit_bytes=VMEM_LIMIT)


def _ada_kernel(c_ref, w_ref, b_ref, o_ref):
    s = _silu(c_ref[...])
    o_ref[...] = _dot3(s, w_ref[...]) + b_ref[...]


def _ada(c_pad, w, b):
    rows, d = c_pad.shape
    n = w.shape[1]
    tn = 1024
    return pl.pallas_call(
        _ada_kernel,
        out_shape=jax.ShapeDtypeStruct((rows, n), F32),
        grid=(n // tn,),
        in_specs=[pl.BlockSpec((rows, d), lambda j: (0, 0)),
                  pl.BlockSpec((d, tn), lambda j: (0, j)),
                  pl.BlockSpec((1, tn), lambda j: (0, j))],
        out_specs=pl.BlockSpec((rows, tn), lambda j: (0, j)),
        compiler_params=_params("arbitrary"),
        name="ada_proj",
    )(c_pad, w, b)


def _norm_modulate(h, gain, shift, scale):
    ms = jnp.mean(h * h, axis=-1, keepdims=True)
    y = h * lax.rsqrt(ms + NORM_EPS) * gain
    return y * (1.0 + scale) + shift


def _ffn_kernel(h_ref, ada_ref, g_ref, w1_ref, w3_ref, w2_ref, o_ref, u_ref, acc_ref, *, slot, nf):
    f = pl.program_id(1)

    @pl.when(f == 0)
    def _():
        u = _norm_modulate(h_ref[...], g_ref[...], ada_ref[0, slot:slot + 1, :], ada_ref[0, slot + 1:slot + 2, :])
        u_ref[...] = u.astype(BF16)
        acc_ref[...] = jnp.zeros_like(acc_ref)

    u = u_ref[...]
    a = jnp.dot(u, w1_ref[...], preferred_element_type=F32)
    b = jnp.dot(u, w3_ref[...], preferred_element_type=F32)
    acc_ref[...] += jnp.dot((_silu(a) * b).astype(BF16), w2_ref[...], preferred_element_type=F32)

    @pl.when(f == nf - 1)
    def _():
        o_ref[...] = h_ref[...] + 0.5 * ada_ref[0, slot + 2:slot + 3, :] * acc_ref[...]


def _ffn(h, ada3, gain, w1, w3, w2, *, slot, seq):
    m, d = h.shape
    dff = w1.shape[1]
    tm, tf = 512, 512
    nf = dff // tf
    per_batch = seq // tm
    return pl.pallas_call(
        functools.partial(_ffn_kernel, slot=slot, nf=nf),
        out_shape=jax.ShapeDtypeStruct((m, d), F32),
        grid=(m // tm, nf),
        in_specs=[pl.BlockSpec((tm, d), lambda i, f: (i, 0)),
                  pl.BlockSpec((1, N_ADA, d), lambda i, f: (i // per_batch, 0, 0)),
                  pl.BlockSpec((1, d), lambda i, f: (0, 0)),
                  pl.BlockSpec((d, tf), lambda i, f: (0, f)),
                  pl.BlockSpec((d, tf), lambda i, f: (0, f)),
                  pl.BlockSpec((tf, d), lambda i, f: (f, 0))],
        out_specs=pl.BlockSpec((tm, d), lambda i, f: (i, 0)),
        scratch_shapes=[pltpu.VMEM((tm, d), BF16), pltpu.VMEM((tm, d), F32)],
        compiler_params=_params("parallel", "arbitrary"),
        name="ffn",
    )(h, ada3, gain, w1, w3, w2)


def _inproj_kernel(h_ref, ada_ref, g_ref, w_ref, wba_ref, o_ref, ba_ref, u_ref):
    j = pl.program_id(1)

    @pl.when(j == 0)
    def _():
        u = _norm_modulate(h_ref[...], g_ref[...], ada_ref[0, 3:4, :], ada_ref[0, 4:5, :])
        u_ref[...] = u.astype(BF16)
        ba_ref[...] = jnp.dot(u_ref[...], wba_ref[...], preferred_element_type=F32)

    o_ref[...] = jnp.dot(u_ref[...], w_ref[...], preferred_element_type=F32)


def _inproj(h, ada3, gain, w, wba, *, seq):
    m, d = h.shape
    n = w.shape[1]
    tm, tn = 512, 1024
    per_batch = seq // tm
    return pl.pallas_call(
        _inproj_kernel,
        out_shape=(jax.ShapeDtypeStruct((m, n), F32), jax.ShapeDtypeStruct((m, LANES), F32)),
        grid=(m // tm, n // tn),
        in_specs=[pl.BlockSpec((tm, d), lambda i, j: (i, 0)),
                  pl.BlockSpec((1, N_ADA, d), lambda i, j: (i // per_batch, 0, 0)),
                  pl.BlockSpec((1, d), lambda i, j: (0, 0)),
                  pl.BlockSpec((d, tn), lambda i, j: (0, j)),
                  pl.BlockSpec((d, LANES), lambda i, j: (0, 0))],
        out_specs=(pl.BlockSpec((tm, tn), lambda i, j: (i, j)),
                   pl.BlockSpec((tm, LANES), lambda i, j: (i, 0))),
        scratch_shapes=[pltpu.VMEM((tm, d), BF16)],
        compiler_params=_params("parallel", "arbitrary"),
        name="in_proj",
    )(h, ada3, gain, w, wba)


def _softplus(x):
    return jnp.maximum(x, 0.0) + jnp.log1p(jnp.exp(-jnp.abs(x)))


def _dn_kernel(q_ref, k_ref, v_ref, z_ref, ba_ref, bat_ref, wq_ref, wk_ref, wv_ref,
               arow_ref, dtrow_ref, acol_ref, dtcol_ref, gain_ref, o_ref,
               prev_ref, state_ref, gcrow_ref, *, hb):
    c = DN_CHUNK
    hg = pl.program_id(1)
    n = pl.program_id(2)

    @pl.when(n == 0)
    def _():
        prev_ref[...] = jnp.zeros_like(prev_ref)
        state_ref[...] = jnp.zeros_like(state_ref)

    width = hb * HEAD_DIM
    row_w = lax.broadcasted_iota(jnp.int32, (c, width), 0)

    def conv_silu(x_ref, w_ref, s):
        cur = x_ref[0]
        prev = prev_ref[s]
        acc = cur * w_ref[DN_CONV - 1:DN_CONV, :]
        for d in range(1, DN_CONV):
            shifted = jnp.where(row_w < d, pltpu.roll(prev, d, 0), pltpu.roll(cur, d, 0))
            acc = acc + shifted * w_ref[DN_CONV - 1 - d:DN_CONV - d, :]
        prev_ref[s] = cur
        return _silu(acc)

    xq = conv_silu(q_ref, wq_ref, 0)
    xk = conv_silu(k_ref, wk_ref, 1)
    xv = conv_silu(v_ref, wv_ref, 2)

    slab = ba_ref[0]
    beta_mat = jax.nn.sigmoid(slab)
    g_mat = -jnp.exp(arow_ref[...]) * _softplus(slab + dtrow_ref[...])
    ri = lax.broadcasted_iota(jnp.int32, (c, c), 0)
    ci = lax.broadcasted_iota(jnp.int32, (c, c), 1)
    lower_incl = (ri >= ci)
    tri_l = jnp.where(lower_incl, 1.0, 0.0).astype(BF16)
    tri_u = jnp.where(ri <= ci, 1.0, 0.0).astype(BF16)

    def exact_ones_dot(x, ones, left):
        x1 = x.astype(BF16)
        r1 = x - x1.astype(F32)
        x2 = r1.astype(BF16)
        x3 = (r1 - x2.astype(F32)).astype(BF16)
        if left:
            f = lambda p: jnp.dot(ones, p, preferred_element_type=F32)
        else:
            f = lambda p: jnp.dot(p, ones, preferred_element_type=F32)
        return f(x1) + (f(x2) + f(x3))

    gc_colmat = exact_ones_dot(g_mat, tri_l, True)
    a_rows = bat_ref[0, HEADS:2 * HEADS, :]
    g_rows = -jnp.exp(acol_ref[...]) * _softplus(a_rows + dtcol_ref[...])
    gcrow_ref[...] = exact_ones_dot(g_rows, tri_u, False)

    lane = lax.broadcasted_iota(jnp.int32, (c, LANES), 1)
    eye = jnp.where(ri == ci, 1.0, 0.0)
    strict = ri > ci
    scale = HEAD_DIM ** -0.5

    for hh in range(hb):
        h = hg * hb + hh
        sl = slice(hh * HEAD_DIM, (hh + 1) * HEAD_DIM)
        beta = jnp.sum(jnp.where(lane == h, beta_mat, 0.0), axis=1, keepdims=True)
        gcc = jnp.sum(jnp.where(lane == HEADS + h, gc_colmat, 0.0), axis=1, keepdims=True)
        gcr = gcrow_ref[pl.ds(h, 1), :]
        gc_last = gcr[:, c - 1:c]
        decay = jnp.where(lower_incl, jnp.exp(jnp.minimum(gcc - gcr, 0.0)), 0.0)

        q = xq[:, sl]
        k = xk[:, sl]
        v = xv[:, sl]
        q = q * (lax.rsqrt(jnp.sum(q * q, axis=-1, keepdims=True) + NORM_EPS) * scale)
        k = k * lax.rsqrt(jnp.sum(k * k, axis=-1, keepdims=True) + NORM_EPS)
        kb = k * beta
        egc = jnp.exp(gcc)

        kk = _dot(kb, k, NT_DIMS)
        nmat = jnp.where(strict, -(kk * decay), 0.0)
        tmat = eye + nmat
        pw = nmat
        for _ in range(int(math.log2(c)) - 1):
            pw = _dot3(pw, pw)
            tmat = tmat + _dot3(tmat, pw)

        rhs = jnp.concatenate([v * beta, kb * egc], axis=1)
        uw = _dot(tmat, rhs)
        u = uw[:, :HEAD_DIM]
        w = uw[:, HEAD_DIM:]
        qk = _dot(q, k, NT_DIMS) * decay
        qd = q * egc
        kd = k * jnp.exp(gc_last - gcc)

        s = state_ref[hh]
        v_new = u - _dot(w, s)
        o = _dot(qd, s) + _dot(qk, v_new)
        state_ref[hh] = s * jnp.exp(gc_last) + _dot(kd, v_new, TN_DIMS)

        o = o * lax.rsqrt(jnp.mean(o * o, axis=-1, keepdims=True) + NORM_EPS) * gain_ref[...]
        o_ref[0, :, sl] = (o * _silu(z_ref[0, :, sl])).astype(o_ref.dtype)


def _deltanet(proj3, ba3, bat3, conv_w, a_log, dt_bias, gain):
    b, t, _ = proj3.shape
    hb = DN_HEAD_BLOCK
    c = DN_CHUNK
    width = hb * HEAD_DIM
    ng = HEADS // hb
    region = (HEADS * HEAD_DIM) // width

    def pad_row(x):
        return jnp.zeros((1, LANES), F32).at[0, HEADS:2 * HEADS].set(x.astype(F32))

    arow, dtrow = pad_row(a_log), pad_row(dt_bias)
    acol = a_log.astype(F32).reshape(HEADS, 1)
    dtcol = dt_bias.astype(F32).reshape(HEADS, 1)
    gain2 = gain.astype(F32).reshape(1, HEAD_DIM)

    def col(r):
        return pl.BlockSpec((1, c, width), lambda bi, g, n, r=r: (bi, n, r * region + g))

    def wcol(r):
        return pl.BlockSpec((DN_CONV, width), lambda bi, g, n, r=r: (0, r * region + g))

    const = lambda shape: pl.BlockSpec(shape, lambda bi, g, n: tuple(0 for _ in shape))
    return pl.pallas_call(
        functools.partial(_dn_kernel, hb=hb),
        out_shape=jax.ShapeDtypeStruct((b, t, HEADS * HEAD_DIM), BF16),
        grid=(b, ng, t // c),
        in_specs=[col(0), col(1), col(2), col(3),
                  pl.BlockSpec((1, c, LANES), lambda bi, g, n: (bi, n, 0)),
                  pl.BlockSpec((1, LANES, c), lambda bi, g, n: (bi, 0, n)),
                  wcol(0), wcol(1), wcol(2),
                  const((1, LANES)), const((1, LANES)), const((HEADS, 1)), const((HEADS, 1)),
                  const((1, HEAD_DIM))],
        out_specs=pl.BlockSpec((1, c, width), lambda bi, g, n: (bi, n, g)),
        scratch_shapes=[pltpu.VMEM((3, c, width), F32),
                        pltpu.VMEM((hb, HEAD_DIM, HEAD_DIM), F32),
                        pltpu.VMEM((HEADS, c), F32)],
        compiler_params=_params("parallel", "parallel", "arbitrary"),
        name="deltanet",
    )(proj3, proj3, proj3, proj3, ba3, bat3, conv_w, conv_w, conv_w, arow, dtrow, acol, dtcol, gain2)


def _moba_prep_kernel(q_ref, k_ref, v_ref, qg_ref, kg_ref, qn_ref, kn_ref, vt_ref, km_ref):
    n = pl.program_id(1)
    scale = HEAD_DIM ** -0.5

    @pl.when(n == 0)
    def _():
        km_ref[...] = jnp.zeros_like(km_ref)

    row = lax.broadcasted_iota(jnp.int32, (km_ref.shape[1], HEAD_DIM), 0)
    for h in range(HEADS):
        sl = slice(h * HEAD_DIM, (h + 1) * HEAD_DIM)
        q = q_ref[0, :, sl]
        k = k_ref[0, :, sl]
        qn = q * lax.rsqrt(jnp.mean(q * q, axis=-1, keepdims=True) + NORM_EPS) * qg_ref[...]
        kn = k * lax.rsqrt(jnp.mean(k * k, axis=-1, keepdims=True) + NORM_EPS) * kg_ref[...]
        qn_ref[0, :, sl] = qn * scale
        kn_ref[0, 0, :, sl] = kn.astype(BF16)
        km_ref[0, :, sl] = jnp.where(row == n, jnp.mean(kn, axis=0, keepdims=True), km_ref[0, :, sl])
        vt_ref[0, 0, sl, :] = v_ref[0, :, sl].T.astype(BF16)


def _moba_prep(proj3, q_gain, k_gain):
    b, t, _ = proj3.shape
    nb = t // MB_BLOCK
    w = HEADS * HEAD_DIM
    col = lambda r: pl.BlockSpec((1, MB_BLOCK, w), lambda bi, n, r=r: (bi, n, r))
    gspec = pl.BlockSpec((1, HEAD_DIM), lambda bi, n: (0, 0))
    return pl.pallas_call(
        _moba_prep_kernel,
        out_shape=(jax.ShapeDtypeStruct((b, t, w), F32),
                   jax.ShapeDtypeStruct((b, nb, MB_BLOCK, w), BF16),
                   jax.ShapeDtypeStruct((b, nb, w, MB_BLOCK), BF16),
                   jax.ShapeDtypeStruct((b, nb, w), F32)),
        grid=(b, nb),
        in_specs=[col(4), col(5), col(6), gspec, gspec],
        out_specs=(pl.BlockSpec((1, MB_BLOCK, w), lambda bi, n: (bi, n, 0)),
                   pl.BlockSpec((1, 1, MB_BLOCK, w), lambda bi, n: (bi, n, 0, 0)),
                   pl.BlockSpec((1, 1, w, MB_BLOCK), lambda bi, n: (bi, n, 0, 0)),
                   pl.BlockSpec((1, nb, w), lambda bi, n: (bi, 0, 0))),
        compiler_params=_params("parallel", "arbitrary"),
        name="moba_prep",
    )(proj3, proj3, proj3, q_gain.astype(F32).reshape(1, HEAD_DIM), k_gain.astype(F32).reshape(1, HEAD_DIM))


def _bucket_thresholds():
    max_exact = RP_BUCKETS // 2
    th = list(range(1, max_exact + 1))
    for kk in range(1, RP_BUCKETS - max_exact):
        x = max_exact * (RP_MAX_DIST / max_exact) ** (kk / (RP_BUCKETS - max_exact))
        r = round(x)
        th.append(r if abs(x - r) < 1e-9 else math.ceil(x))
    return th


N_TABLES = 6


def _moba_kernel(rb_ref, q_ref, k_ref, vt_ref, km_ref, o_ref, tab_ref, sel_ref, *, nb):
    h = pl.program_id(0)
    bi = pl.program_id(1)
    i = pl.program_id(2)
    blk = MB_BLOCK
    thresholds = _bucket_thresholds()

    @pl.when(jnp.logical_and(bi == 0, i == 0))
    def _():
        ko = lax.broadcasted_iota(jnp.int32, (blk, blk), 0)
        qo = lax.broadcasted_iota(jnp.int32, (blk, blk), 1)
        for delta in range(N_TABLES - 1):
            d = delta * blk + qo - ko
            acc = jnp.full((blk, blk), rb_ref[0, h], F32)
            for bkt in range(1, RP_BUCKETS):
                if delta * blk + blk - 1 >= thresholds[bkt - 1]:
                    acc = jnp.where(d >= thresholds[bkt - 1], rb_ref[bkt, h], acc)
            if delta == 0:
                acc = jnp.where(d >= 0, acc, NEG_BIG)
            tab_ref[delta] = acc
        tab_ref[N_TABLES - 1] = jnp.full((blk, blk), rb_ref[RP_BUCKETS - 1, h], F32)

    q = q_ref[0]
    gate = _dot3(km_ref[0], q, NT_DIMS)
    bidx = lax.broadcasted_iota(jnp.int32, (nb, blk), 0)
    past = bidx < i
    gate = jnp.where(past, gate, -jnp.inf)
    rank = jnp.zeros((nb, blk), jnp.int32)
    for jp in range(nb):
        gj = gate[jp:jp + 1, :]
        ahead = jnp.logical_or(gj > gate, jnp.logical_and(gj == gate, jp < bidx))
        rank = rank + ahead.astype(jnp.int32)
    selected = jnp.logical_and(past, rank < MB_TOPK)
    sel_ref[...] = jnp.where(selected, 0.0, NEG_BIG)

    qb = q.astype(BF16)

    def body(t, carry):
        m, l, acc = carry
        j = i - t
        s = lax.dot_general(k_ref[0, j], qb, NT_DIMS, preferred_element_type=F32)
        s = s + tab_ref[jnp.minimum(t, N_TABLES - 1)]
        s = s + jnp.where(t == 0, 0.0, sel_ref[pl.ds(j, 1), :])
        m_new = jnp.maximum(m, jnp.max(s, axis=0, keepdims=True))
        alpha = jnp.exp(m - m_new)
        p = jnp.exp(s - m_new)
        l = alpha * l + jnp.sum(p, axis=0, keepdims=True)
        acc = alpha * acc + jnp.dot(vt_ref[0, j], p.astype(BF16), preferred_element_type=F32)
        return m_new, l, acc

    init = (jnp.full((1, blk), NEG_BIG, F32), jnp.zeros((1, blk), F32), jnp.zeros((HEAD_DIM, blk), F32))
    _, l, acc = lax.fori_loop(0, i + 1, body, init)
    o_ref[0] = (acc / l).T.astype(o_ref.dtype)


def _moba(qn, kn, vt, kmean, rel_bias):
    b, t, w = qn.shape
    nb = t // MB_BLOCK
    return pl.pallas_call(
        functools.partial(_moba_kernel, nb=nb),
        out_shape=jax.ShapeDtypeStruct((b, t, w), BF16),
        grid=(HEADS, b, nb),
        in_specs=[pl.BlockSpec(memory_space=pltpu.SMEM),
                  pl.BlockSpec((1, MB_BLOCK, HEAD_DIM), lambda h, bi, i: (bi, i, h)),
                  pl.BlockSpec((1, nb, MB_BLOCK, HEAD_DIM), lambda h, bi, i: (bi, 0, 0, h)),
                  pl.BlockSpec((1, nb, HEAD_DIM, MB_BLOCK), lambda h, bi, i: (bi, 0, h, 0)),
                  pl.BlockSpec((1, nb, HEAD_DIM), lambda h, bi, i: (bi, 0, h))],
        out_specs=pl.BlockSpec((1, MB_BLOCK, HEAD_DIM), lambda h, bi, i: (bi, i, h)),
        scratch_shapes=[pltpu.VMEM((N_TABLES, MB_BLOCK, MB_BLOCK), F32),
                        pltpu.VMEM((nb, MB_BLOCK), F32)],
        compiler_params=_params("arbitrary", "arbitrary", "arbitrary"),
        name="moba_attn",
    )(rel_bias.astype(F32), qn, kn, vt, kmean)


def _merge_kernel(ya_ref, yb_ref, wa_ref, wb_ref, ga_ref, gb_ref, o_ref):
    pa = jnp.dot(ya_ref[...], wa_ref[...], preferred_element_type=F32)
    pb = jnp.dot(yb_ref[...], wb_ref[...], preferred_element_type=F32)
    o_ref[...] = (jax.nn.sigmoid(ga_ref[...]) * pa + jax.nn.sigmoid(gb_ref[...]) * pb).astype(o_ref.dtype)


def _merge(ya, yb, wa, wb, proj, gate_col):
    m, d = ya.shape
    tm, tn = 1024, 512
    gpb = d // tn
    return pl.pallas_call(
        _merge_kernel,
        out_shape=jax.ShapeDtypeStruct((m, d), BF16),
        grid=(m // tm, d // tn),
        in_specs=[pl.BlockSpec((tm, d), lambda i, j: (i, 0)),
                  pl.BlockSpec((tm, d), lambda i, j: (i, 0)),
                  pl.BlockSpec((d, tn), lambda i, j: (0, j)),
                  pl.BlockSpec((d, tn), lambda i, j: (0, j)),
                  pl.BlockSpec((tm, tn), lambda i, j: (i, gate_col * gpb + j)),
                  pl.BlockSpec((tm, tn), lambda i, j: (i, (gate_col + 1) * gpb + j))],
        out_specs=pl.BlockSpec((tm, tn), lambda i, j: (i, j)),
        compiler_params=_params("parallel", "arbitrary"),
        name="merge",
    )(ya, yb, wa, wb, proj, proj)


def _outproj_kernel(x_ref, w_ref, h_ref, ada_ref, o_ref):
    o_ref[...] = h_ref[...] + ada_ref[0] * jnp.dot(x_ref[...], w_ref[...], preferred_element_type=F32)


def _outproj(x, w, h, gate3, *, seq):
    m, d = x.shape
    tm, tn = 1024, 512
    per_batch = seq // tm
    return pl.pallas_call(
        _outproj_kernel,
        out_shape=jax.ShapeDtypeStruct((m, d), F32),
        grid=(m // tm, d // tn),
        in_specs=[pl.BlockSpec((tm, d), lambda i, j: (i, 0)),
                  pl.BlockSpec((d, tn), lambda i, j: (0, j)),
                  pl.BlockSpec((tm, tn), lambda i, j: (i, j)),
                  pl.BlockSpec((1, 1, tn), lambda i, j: (i // per_batch, 0, j))],
        out_specs=pl.BlockSpec((tm, tn), lambda i, j: (i, j)),
        compiler_params=_params("parallel", "arbitrary"),
        name="out_proj",
    )(x, w, h, gate3)


def kernel(x, c, ada_w, ada_b, norm1_g, ffn1_w1, ffn1_w3, ffn1_w2, norm2_g, w_in, dn_conv_w, dn_a_log, dn_dt_bias, dn_norm_g, mb_q_norm_g, mb_k_norm_g, rel_bias, w_proj_a, w_proj_b, w_out, norm3_g, ffn2_w1, ffn2_w3, ffn2_w2):
    b, t, d = x.shape
    m = b * t
    depth = ada_w.shape[0]
    wide = HEADS * HEAD_DIM
    cut0 = 4 * wide
    cut1 = cut0 + 2 * HEADS
    h = x.reshape(m, d)
    c_pad = jnp.zeros((8, d), F32).at[:b].set(c)
    for l in range(depth):
        ada = _ada(c_pad, ada_w[l], ada_b[l].reshape(1, -1))[:b]
        ada3 = ada.reshape(b, N_ADA, d)
        h = _ffn(h, ada3, norm1_g[l].reshape(1, d), ffn1_w1[l].astype(BF16), ffn1_w3[l].astype(BF16),
                 ffn1_w2[l].astype(BF16), slot=0, seq=t)

        w_main = jnp.concatenate([w_in[l][:, :cut0], w_in[l][:, cut1:]], axis=1).astype(BF16)
        w_ba = jnp.zeros((d, LANES), BF16).at[:, :2 * HEADS].set(w_in[l][:, cut0:cut1].astype(BF16))
        proj, ba = _inproj(h, ada3, norm2_g[l].reshape(1, d), w_main, w_ba, seq=t)
        proj3 = proj.reshape(b, t, -1)
        ba3 = ba.reshape(b, t, LANES)
        bat3 = jnp.transpose(ba3, (0, 2, 1))
        y_a = _deltanet(proj3, ba3, bat3, dn_conv_w[l], dn_a_log[l], dn_dt_bias[l], dn_norm_g[l])
        qn, kn, vt, kmean = _moba_prep(proj3, mb_q_norm_g[l], mb_k_norm_g[l])
        y_b = _moba(qn, kn, vt, kmean, rel_bias)
        merged = _merge(y_a.reshape(m, wide), y_b.reshape(m, wide), w_proj_a[l].astype(BF16),
                        w_proj_b[l].astype(BF16), proj, 7)
        gate2 = ada3[:, 5:6, :]
        h = _outproj(merged, w_out[l].astype(BF16), h, gate2, seq=t)

        h = _ffn(h, ada3, norm3_g[l].reshape(1, d), ffn2_w1[l].astype(BF16), ffn2_w3[l].astype(BF16),
                 ffn2_w2[l].astype(BF16), slot=6, seq=t)
    return h.reshape(b, t, d)
```

```python
import functools
import math

import jax
import jax.numpy as jnp
from jax import lax
from jax.experimental import pallas as pl
from jax.experimental.pallas import tpu as pltpu

F32 = jnp.float32
BF16 = jnp.bfloat16

LANES = 128
NORM_EPS = 1e-6
N_ADA = 9
HEADS = 16
HEAD_DIM = 128
DN_CONV = 4
DN_CHUNK = 128
DN_HEAD_BLOCK = 4
DN_INV_BASE = 8
MB_BLOCK = 256
MB_TOPK = 3
MB_HEAD_BLOCK = 4
RP_BUCKETS = 32
RP_MAX_DIST = 1024
NEG_BIG = -1e30
VMEM_LIMIT = 56 * 1024 * 1024

NT_DIMS = (((1,), (1,)), ((), ()))
NN_DIMS = (((1,), (0,)), ((), ()))
TN_DIMS = (((0,), (0,)), ((), ()))


def _dot(a, b, dims=NN_DIMS):
    return lax.dot_general(a.astype(BF16), b.astype(BF16), dims, preferred_element_type=F32)


def _split_bf16(x):
    hi = x.astype(BF16)
    lo = (x - hi.astype(F32)).astype(BF16)
    return hi, lo


def _dot3(a, b, dims=NN_DIMS):
    ah, al = _split_bf16(a)
    bh, bl = _split_bf16(b)
    f = lambda x, y: lax.dot_general(x, y, dims, preferred_element_type=F32)
    return f(ah, bh) + (f(ah, bl) + f(al, bh))


def _silu(x):
    return x * jax.nn.sigmoid(x)


def _params(*sem):
    return pltpu.CompilerParams(dimension_semantics=sem, vmem_limit_bytes=VMEM_LIMIT)


def _ada_kernel(c_ref, w_ref, b_ref, o_ref):
    s = _silu(c_ref[...])
    o_ref[...] = _dot3(s, w_ref[...]) + b_ref[...]


def _ada(c_pad, w, b):
    rows, d = c_pad.shape
    n = w.shape[1]
    tn = 1024
    return pl.pallas_call(
        _ada_kernel,
        out_shape=jax.ShapeDtypeStruct((rows, n), F32),
        grid=(n // tn,),
        in_specs=[pl.BlockSpec((rows, d), lambda j: (0, 0)),
                  pl.BlockSpec((d, tn), lambda j: (0, j)),
                  pl.BlockSpec((1, tn), lambda j: (0, j))],
        out_specs=pl.BlockSpec((rows, tn), lambda j: (0, j)),
        compiler_params=_params("arbitrary"),
        name="ada_proj",
    )(c_pad, w, b)


def _norm_modulate(h, gain, shift, scale):
    ms = jnp.mean(h * h, axis=-1, keepdims=True)
    y = h * lax.rsqrt(ms + NORM_EPS) * gain
    return y * (1.0 + scale) + shift


def _ffn_kernel(h_ref, ada_ref, g_ref, w1_ref, w3_ref, w2_ref, o_ref, u_ref, acc_ref, *, slot, nf):
    f = pl.program_id(1)

    @pl.when(f == 0)
    def _():
        u = _norm_modulate(h_ref[...], g_ref[...], ada_ref[0, slot:slot + 1, :], ada_ref[0, slot + 1:slot + 2, :])
        u_ref[...] = u.astype(BF16)
        acc_ref[...] = jnp.zeros_like(acc_ref)

    u = u_ref[...]
    a = jnp.dot(u, w1_ref[...], preferred_element_type=F32)
    b = jnp.dot(u, w3_ref[...], preferred_element_type=F32)
    acc_ref[...] += jnp.dot((_silu(a) * b).astype(BF16), w2_ref[...], preferred_element_type=F32)

    @pl.when(f == nf - 1)
    def _():
        o_ref[...] = h_ref[...] + 0.5 * ada_ref[0, slot + 2:slot + 3, :] * acc_ref[...]


def _ffn(h, ada3, gain, w1, w3, w2, *, slot, seq):
    m, d = h.shape
    dff = w1.shape[1]
    tm, tf = 512, 512
    nf = dff // tf
    per_batch = seq // tm
    return pl.pallas_call(
        functools.partial(_ffn_kernel, slot=slot, nf=nf),
        out_shape=jax.ShapeDtypeStruct((m, d), F32),
        grid=(m // tm, nf),
        in_specs=[pl.BlockSpec((tm, d), lambda i, f: (i, 0)),
                  pl.BlockSpec((1, N_ADA, d), lambda i, f: (i // per_batch, 0, 0)),
                  pl.BlockSpec((1, d), lambda i, f: (0, 0)),
                  pl.BlockSpec((d, tf), lambda i, f: (0, f)),
                  pl.BlockSpec((d, tf), lambda i, f: (0, f)),
                  pl.BlockSpec((tf, d), lambda i, f: (f, 0))],
        out_specs=pl.BlockSpec((tm, d), lambda i, f: (i, 0)),
        scratch_shapes=[pltpu.VMEM((tm, d), BF16), pltpu.VMEM((tm, d), F32)],
        compiler_params=_params("parallel", "arbitrary"),
        name="ffn",
    )(h, ada3, gain, w1, w3, w2)


def _inproj_kernel(h_ref, ada_ref, g_ref, w_ref, wba_ref, o_ref, ba_ref, u_ref):
    j = pl.program_id(1)

    @pl.when(j == 0)
    def _():
        u = _norm_modulate(h_ref[...], g_ref[...], ada_ref[0, 3:4, :], ada_ref[0, 4:5, :])
        u_ref[...] = u.astype(BF16)
        ba_ref[...] = jnp.dot(u_ref[...], wba_ref[...], preferred_element_type=F32)

    o_ref[...] = jnp.dot(u_ref[...], w_ref[...], preferred_element_type=F32)


def _inproj(h, ada3, gain, w, wba, *, seq):
    m, d = h.shape
    n = w.shape[1]
    tm, tn = 512, 1024
    per_batch = seq // tm
    return pl.pallas_call(
        _inproj_kernel,
        out_shape=(jax.ShapeDtypeStruct((m, n), F32), jax.ShapeDtypeStruct((m, LANES), F32)),
        grid=(m // tm, n // tn),
        in_specs=[pl.BlockSpec((tm, d), lambda i, j: (i, 0)),
                  pl.BlockSpec((1, N_ADA, d), lambda i, j: (i // per_batch, 0, 0)),
                  pl.BlockSpec((1, d), lambda i, j: (0, 0)),
                  pl.BlockSpec((d, tn), lambda i, j: (0, j)),
                  pl.BlockSpec((d, LANES), lambda i, j: (0, 0))],
        out_specs=(pl.BlockSpec((tm, tn), lambda i, j: (i, j)),
                   pl.BlockSpec((tm, LANES), lambda i, j: (i, 0))),
        scratch_shapes=[pltpu.VMEM((tm, d), BF16)],
        compiler_params=_params("parallel", "arbitrary"),
        name="in_proj",
    )(h, ada3, gain, w, wba)


def _softplus(x):
    return jnp.maximum(x, 0.0) + jnp.log1p(jnp.exp(-jnp.abs(x)))


def _dn_kernel(q_ref, k_ref, v_ref, z_ref, ba_ref, bat_ref, wq_ref, wk_ref, wv_ref,
               arow_ref, dtrow_ref, acol_ref, dtcol_ref, gain_ref, o_ref,
               prev_ref, state_ref, gcrow_ref, *, hb):
    c = DN_CHUNK
    hg = pl.program_id(1)
    n = pl.program_id(2)

    @pl.when(n == 0)
    def _():
        prev_ref[...] = jnp.zeros_like(prev_ref)
        state_ref[...] = jnp.zeros_like(state_ref)

    width = hb * HEAD_DIM
    row_w = lax.broadcasted_iota(jnp.int32, (c, width), 0)

    def conv_silu(x_ref, w_ref, s):
        cur = x_ref[0]
        prev = prev_ref[s]
        acc = cur * w_ref[DN_CONV - 1:DN_CONV, :]
        for d in range(1, DN_CONV):
            shifted = jnp.where(row_w < d, pltpu.roll(prev, d, 0), pltpu.roll(cur, d, 0))
            acc = acc + shifted * w_ref[DN_CONV - 1 - d:DN_CONV - d, :]
        prev_ref[s] = cur
        return _silu(acc)

    xq = conv_silu(q_ref, wq_ref, 0)
    xk = conv_silu(k_ref, wk_ref, 1)
    xv = conv_silu(v_ref, wv_ref, 2)

    slab = ba_ref[0]
    beta_mat = jax.nn.sigmoid(slab)
    g_mat = -jnp.exp(arow_ref[...]) * _softplus(slab + dtrow_ref[...])
    ri = lax.broadcasted_iota(jnp.int32, (c, c), 0)
    ci = lax.broadcasted_iota(jnp.int32, (c, c), 1)
    lower_incl = (ri >= ci)
    tri_l = jnp.where(lower_incl, 1.0, 0.0).astype(BF16)
    tri_u = jnp.where(ri <= ci, 1.0, 0.0).astype(BF16)

    def exact_ones_dot(x, ones, left):
        x1 = x.astype(BF16)
        r1 = x - x1.astype(F32)
        x2 = r1.astype(BF16)
        x3 = (r1 - x2.astype(F32)).astype(BF16)
        if left:
            f = lambda p: jnp.dot(ones, p, preferred_element_type=F32)
        else:
            f = lambda p: jnp.dot(p, ones, preferred_element_type=F32)
        return f(x1) + (f(x2) + f(x3))

    gc_colmat = exact_ones_dot(g_mat, tri_l, True)
    a_rows = bat_ref[0, HEADS:2 * HEADS, :]
    g_rows = -jnp.exp(acol_ref[...]) * _softplus(a_rows + dtcol_ref[...])
    gcrow_ref[...] = exact_ones_dot(g_rows, tri_u, False)

    lane = lax.broadcasted_iota(jnp.int32, (c, LANES), 1)
    eye = jnp.where(ri == ci, 1.0, 0.0)
    strict = ri > ci
    scale = HEAD_DIM ** -0.5

    hs = []
    for hh in range(hb):
        h = hg * hb + hh
        sl = slice(hh * HEAD_DIM, (hh + 1) * HEAD_DIM)
        beta = jnp.sum(jnp.where(lane == h, beta_mat, 0.0), axis=1, keepdims=True)
        gcc = jnp.sum(jnp.where(lane == HEADS + h, gc_colmat, 0.0), axis=1, keepdims=True)
        gcr = gcrow_ref[pl.ds(h, 1), :]
        gc_last = gcr[:, c - 1:c]
        decay = jnp.where(lower_incl, jnp.exp(jnp.minimum(gcc - gcr, 0.0)), 0.0)
        q = xq[:, sl]
        k = xk[:, sl]
        q = q * (lax.rsqrt(jnp.sum(q * q, axis=-1, keepdims=True) + NORM_EPS) * scale)
        k = k * lax.rsqrt(jnp.sum(k * k, axis=-1, keepdims=True) + NORM_EPS)
        kb = k * beta
        egc = jnp.exp(gcc)
        hs.append(dict(hh=hh, sl=sl, decay=decay, q=q, k=k, kb=kb,
                       rhs=jnp.concatenate([xv[:, sl] * beta, kb * egc], axis=1),
                       qd=q * egc, kd=k * jnp.exp(gc_last - gcc), g_tot=jnp.exp(gc_last)))

    for s in hs:
        s["lmat"] = jnp.where(strict, _dot(s["kb"], s["k"], NT_DIMS) * s["decay"], 0.0)
        s["qk"] = _dot(s["q"], s["k"], NT_DIMS) * s["decay"]
    shift0 = int(math.log2(DN_INV_BASE))
    same_block = (ri >> shift0) == (ci >> shift0)
    for s in hs:
        s["pw"] = jnp.where(same_block, -s["lmat"], 0.0)
        s["tmat"] = eye + s["pw"]
    for _ in range(shift0 - 1):
        for s in hs:
            s["pw"] = _dot3(s["pw"], s["pw"])
        for s in hs:
            s["tmat"] = s["tmat"] + _dot3(s["tmat"], s["pw"])
    for sh in range(shift0, int(math.log2(c))):
        lower_left = jnp.logical_and((ri >> sh) == (ci >> sh) + 1, ((ci >> sh) & 1) == 0)
        for s in hs:
            s["pw"] = _dot3(jnp.where(lower_left, s["lmat"], 0.0), s["tmat"])
        for s in hs:
            s["tmat"] = s["tmat"] - _dot3(s["tmat"], s["pw"])
    for s in hs:
        s["uw"] = _dot(s["tmat"], s["rhs"])
    for s in hs:
        s["state"] = state_ref[s["hh"]]
        s["v_new"] = s["uw"][:, :HEAD_DIM] - _dot(s["uw"][:, HEAD_DIM:], s["state"])
    for s in hs:
        s["o"] = _dot(s["qd"], s["state"]) + _dot(s["qk"], s["v_new"])
        state_ref[s["hh"]] = s["state"] * s["g_tot"] + _dot(s["kd"], s["v_new"], TN_DIMS)
    for s in hs:
        o = s["o"]
        o = o * lax.rsqrt(jnp.mean(o * o, axis=-1, keepdims=True) + NORM_EPS) * gain_ref[...]
        o_ref[0, :, s["sl"]] = (o * _silu(z_ref[0, :, s["sl"]])).astype(o_ref.dtype)


def _deltanet(proj3, ba3, bat3, conv_w, a_log, dt_bias, gain):
    b, t, _ = proj3.shape
    hb = DN_HEAD_BLOCK
    c = DN_CHUNK
    width = hb * HEAD_DIM
    ng = HEADS // hb
    region = (HEADS * HEAD_DIM) // width

    def pad_row(x):
        return jnp.zeros((1, LANES), F32).at[0, HEADS:2 * HEADS].set(x.astype(F32))

    arow, dtrow = pad_row(a_log), pad_row(dt_bias)
    acol = a_log.astype(F32).reshape(HEADS, 1)
    dtcol = dt_bias.astype(F32).reshape(HEADS, 1)
    gain2 = gain.astype(F32).reshape(1, HEAD_DIM)

    def col(r):
        return pl.BlockSpec((1, c, width), lambda bi, g, n, r=r: (bi, n, r * region + g))

    def wcol(r):
        return pl.BlockSpec((DN_CONV, width), lambda bi, g, n, r=r: (0, r * region + g))

    const = lambda shape: pl.BlockSpec(shape, lambda bi, g, n: tuple(0 for _ in shape))
    return pl.pallas_call(
        functools.partial(_dn_kernel, hb=hb),
        out_shape=jax.ShapeDtypeStruct((b, t, HEADS * HEAD_DIM), BF16),
        grid=(b, ng, t // c),
        in_specs=[col(0), col(1), col(2), col(3),
                  pl.BlockSpec((1, c, LANES), lambda bi, g, n: (bi, n, 0)),
                  pl.BlockSpec((1, LANES, c), lambda bi, g, n: (bi, 0, n)),
                  wcol(0), wcol(1), wcol(2),
                  const((1, LANES)), const((1, LANES)), const((HEADS, 1)), const((HEADS, 1)),
                  const((1, HEAD_DIM))],
        out_specs=pl.BlockSpec((1, c, width), lambda bi, g, n: (bi, n, g)),
        scratch_shapes=[pltpu.VMEM((3, c, width), F32),
                        pltpu.VMEM((hb, HEAD_DIM, HEAD_DIM), F32),
                        pltpu.VMEM((HEADS, c), F32)],
        compiler_params=_params("parallel", "parallel", "arbitrary"),
        name="deltanet",
    )(proj3, proj3, proj3, proj3, ba3, bat3, conv_w, conv_w, conv_w, arow, dtrow, acol, dtcol, gain2)


def _moba_prep_kernel(q_ref, k_ref, v_ref, qg_ref, kg_ref, qn_ref, kn_ref, vt_ref, km_ref):
    n = pl.program_id(1)
    scale = HEAD_DIM ** -0.5

    @pl.when(n == 0)
    def _():
        km_ref[...] = jnp.zeros_like(km_ref)

    row = lax.broadcasted_iota(jnp.int32, (km_ref.shape[1], HEAD_DIM), 0)
    for h in range(HEADS):
        sl = slice(h * HEAD_DIM, (h + 1) * HEAD_DIM)
        q = q_ref[0, :, sl]
        k = k_ref[0, :, sl]
        qn = q * lax.rsqrt(jnp.mean(q * q, axis=-1, keepdims=True) + NORM_EPS) * qg_ref[...]
        kn = k * lax.rsqrt(jnp.mean(k * k, axis=-1, keepdims=True) + NORM_EPS) * kg_ref[...]
        qn_ref[0, :, sl] = qn * scale
        kn_ref[0, 0, :, sl] = kn.astype(BF16)
        km_ref[0, :, sl] = jnp.where(row == n, jnp.mean(kn, axis=0, keepdims=True), km_ref[0, :, sl])
        vt_ref[0, 0, sl, :] = v_ref[0, :, sl].T.astype(BF16)


def _moba_prep(proj3, q_gain, k_gain):
    b, t, _ = proj3.shape
    nb = t // MB_BLOCK
    w = HEADS * HEAD_DIM
    col = lambda r: pl.BlockSpec((1, MB_BLOCK, w), lambda bi, n, r=r: (bi, n, r))
    gspec = pl.BlockSpec((1, HEAD_DIM), lambda bi, n: (0, 0))
    return pl.pallas_call(
        _moba_prep_kernel,
        out_shape=(jax.ShapeDtypeStruct((b, t, w), F32),
                   jax.ShapeDtypeStruct((b, nb, MB_BLOCK, w), BF16),
                   jax.ShapeDtypeStruct((b, nb, w, MB_BLOCK), BF16),
                   jax.ShapeDtypeStruct((b, nb, w), F32)),
        grid=(b, nb),
        in_specs=[col(4), col(5), col(6), gspec, gspec],
        out_specs=(pl.BlockSpec((1, MB_BLOCK, w), lambda bi, n: (bi, n, 0)),
                   pl.BlockSpec((1, 1, MB_BLOCK, w), lambda bi, n: (bi, n, 0, 0)),
                   pl.BlockSpec((1, 1, w, MB_BLOCK), lambda bi, n: (bi, n, 0, 0)),
                   pl.BlockSpec((1, nb, w), lambda bi, n: (bi, 0, 0))),
        compiler_params=_params("parallel", "arbitrary"),
        name="moba_prep",
    )(proj3, proj3, proj3, q_gain.astype(F32).reshape(1, HEAD_DIM), k_gain.astype(F32).reshape(1, HEAD_DIM))


def _bucket_thresholds():
    max_exact = RP_BUCKETS // 2
    th = list(range(1, max_exact + 1))
    for kk in range(1, RP_BUCKETS - max_exact):
        x = max_exact * (RP_MAX_DIST / max_exact) ** (kk / (RP_BUCKETS - max_exact))
        r = round(x)
        th.append(r if abs(x - r) < 1e-9 else math.ceil(x))
    return th


N_TABLES = 6


def _moba_kernel(rb_ref, q_ref, k_ref, vt_ref, km_ref, o_ref, tab_ref, sel_ref, qb_ref, m_ref, l_ref, acc_ref,
                 *, nb, hb):
    hg = pl.program_id(0)
    bi = pl.program_id(1)
    i = pl.program_id(2)
    blk = MB_BLOCK
    thresholds = _bucket_thresholds()
    heads = [(hh, slice(hh * HEAD_DIM, (hh + 1) * HEAD_DIM)) for hh in range(hb)]

    @pl.when(jnp.logical_and(bi == 0, i == 0))
    def _():
        ko = lax.broadcasted_iota(jnp.int32, (blk, blk), 0)
        qo = lax.broadcasted_iota(jnp.int32, (blk, blk), 1)
        for hh, _ in heads:
            h = hg * hb + hh
            for delta in range(N_TABLES - 1):
                d = delta * blk + qo - ko
                acc = jnp.full((blk, blk), rb_ref[0, h], F32)
                for bkt in range(1, RP_BUCKETS):
                    if delta * blk + blk - 1 >= thresholds[bkt - 1]:
                        acc = jnp.where(d >= thresholds[bkt - 1], rb_ref[bkt, h], acc)
                if delta == 0:
                    acc = jnp.where(d >= 0, acc, NEG_BIG)
                tab_ref[hh, delta] = acc
            tab_ref[hh, N_TABLES - 1] = jnp.full((blk, blk), rb_ref[RP_BUCKETS - 1, h], F32)

    bidx = lax.broadcasted_iota(jnp.int32, (nb, blk), 0)
    past = bidx < i
    gates = []
    for hh, sl in heads:
        q = q_ref[0, :, sl]
        qb_ref[hh] = q.astype(BF16)
        gate = _dot3(km_ref[0, :, sl], q, NT_DIMS)
        gates.append(jnp.where(past, gate, -jnp.inf))
    for (hh, _), gate in zip(heads, gates):
        rank = jnp.zeros((nb, blk), jnp.int32)
        for jp in range(nb):
            gj = gate[jp:jp + 1, :]
            ahead = jnp.logical_or(gj > gate, jnp.logical_and(gj == gate, jp < bidx))
            rank = rank + ahead.astype(jnp.int32)
        selected = jnp.logical_and(past, rank < MB_TOPK)
        sel_ref[hh] = jnp.where(selected, 0.0, NEG_BIG)

    m_ref[...] = jnp.full(m_ref.shape, NEG_BIG, F32)
    l_ref[...] = jnp.zeros(l_ref.shape, F32)
    acc_ref[...] = jnp.zeros(acc_ref.shape, F32)

    def body(t, carry):
        j = i - t
        tab_idx = jnp.minimum(t, N_TABLES - 1)
        ss = [lax.dot_general(k_ref[0, j, :, sl], qb_ref[hh], NT_DIMS, preferred_element_type=F32)
              for hh, sl in heads]
        ss = [s + tab_ref[hh, tab_idx] + jnp.where(t == 0, 0.0, sel_ref[hh, pl.ds(j, 1), :])
              for (hh, _), s in zip(heads, ss)]
        ps = []
        for (hh, _), s in zip(heads, ss):
            m_old = m_ref[hh]
            m_new = jnp.maximum(m_old, jnp.max(s, axis=0, keepdims=True))
            alpha = jnp.exp(m_old - m_new)
            p = jnp.exp(s - m_new)
            l_ref[hh] = alpha * l_ref[hh] + jnp.sum(p, axis=0, keepdims=True)
            m_ref[hh] = m_new
            ps.append((alpha, p.astype(BF16)))
        for (hh, sl), (alpha, p) in zip(heads, ps):
            acc_ref[hh] = alpha * acc_ref[hh] + jnp.dot(vt_ref[0, j, sl, :], p, preferred_element_type=F32)
        return carry

    lax.fori_loop(0, i + 1, body, 0)
    for hh, sl in heads:
        o_ref[0, :, sl] = (acc_ref[hh] / l_ref[hh]).T.astype(o_ref.dtype)


def _moba(qn, kn, vt, kmean, rel_bias):
    b, t, w = qn.shape
    nb = t // MB_BLOCK
    hb = MB_HEAD_BLOCK
    hw = hb * HEAD_DIM
    return pl.pallas_call(
        functools.partial(_moba_kernel, nb=nb, hb=hb),
        out_shape=jax.ShapeDtypeStruct((b, t, w), BF16),
        grid=(HEADS // hb, b, nb),
        in_specs=[pl.BlockSpec(memory_space=pltpu.SMEM),
                  pl.BlockSpec((1, MB_BLOCK, hw), lambda g, bi, i: (bi, i, g)),
                  pl.BlockSpec((1, nb, MB_BLOCK, hw), lambda g, bi, i: (bi, 0, 0, g)),
                  pl.BlockSpec((1, nb, hw, MB_BLOCK), lambda g, bi, i: (bi, 0, g, 0)),
                  pl.BlockSpec((1, nb, hw), lambda g, bi, i: (bi, 0, g))],
        out_specs=pl.BlockSpec((1, MB_BLOCK, hw), lambda g, bi, i: (bi, i, g)),
        scratch_shapes=[pltpu.VMEM((hb, N_TABLES, MB_BLOCK, MB_BLOCK), F32),
                        pltpu.VMEM((hb, nb, MB_BLOCK), F32),
                        pltpu.VMEM((hb, MB_BLOCK, HEAD_DIM), BF16),
                        pltpu.VMEM((hb, 1, MB_BLOCK), F32),
                        pltpu.VMEM((hb, 1, MB_BLOCK), F32),
                        pltpu.VMEM((hb, HEAD_DIM, MB_BLOCK), F32)],
        compiler_params=_params("arbitrary", "arbitrary", "arbitrary"),
        name="moba_attn",
    )(rel_bias.astype(F32), qn, kn, vt, kmean)


def _merge_kernel(ya_ref, yb_ref, wa_ref, wb_ref, ga_ref, gb_ref, o_ref):
    pa = jnp.dot(ya_ref[...], wa_ref[...], preferred_element_type=F32)
    pb = jnp.dot(yb_ref[...], wb_ref[...], preferred_element_type=F32)
    o_ref[...] = (jax.nn.sigmoid(ga_ref[...]) * pa + jax.nn.sigmoid(gb_ref[...]) * pb).astype(o_ref.dtype)


def _merge(ya, yb, wa, wb, proj, gate_col):
    m, d = ya.shape
    tm, tn = 1024, 512
    gpb = d // tn
    return pl.pallas_call(
        _merge_kernel,
        out_shape=jax.ShapeDtypeStruct((m, d), BF16),
        grid=(m // tm, d // tn),
        in_specs=[pl.BlockSpec((tm, d), lambda i, j: (i, 0)),
                  pl.BlockSpec((tm, d), lambda i, j: (i, 0)),
                  pl.BlockSpec((d, tn), lambda i, j: (0, j)),
                  pl.BlockSpec((d, tn), lambda i, j: (0, j)),
                  pl.BlockSpec((tm, tn), lambda i, j: (i, gate_col * gpb + j)),
                  pl.BlockSpec((tm, tn), lambda i, j: (i, (gate_col + 1) * gpb + j))],
        out_specs=pl.BlockSpec((tm, tn), lambda i, j: (i, j)),
        compiler_params=_params("parallel", "arbitrary"),
        name="merge",
    )(ya, yb, wa, wb, proj, proj)


def _outproj_kernel(x_ref, w_ref, h_ref, ada_ref, o_ref):
    o_ref[...] = h_ref[...] + ada_ref[0] * jnp.dot(x_ref[...], w_ref[...], preferred_element_type=F32)


def _outproj(x, w, h, gate3, *, seq):
    m, d = x.shape
    tm, tn = 1024, 512
    per_batch = seq // tm
    return pl.pallas_call(
        _outproj_kernel,
        out_shape=jax.ShapeDtypeStruct((m, d), F32),
        grid=(m // tm, d // tn),
        in_specs=[pl.BlockSpec((tm, d), lambda i, j: (i, 0)),
                  pl.BlockSpec((d, tn), lambda i, j: (0, j)),
                  pl.BlockSpec((tm, tn), lambda i, j: (i, j)),
                  pl.BlockSpec((1, 1, tn), lambda i, j: (i // per_batch, 0, j))],
        out_specs=pl.BlockSpec((tm, tn), lambda i, j: (i, j)),
        compiler_params=_params("parallel", "arbitrary"),
        name="out_proj",
    )(x, w, h, gate3)


def kernel(x, c, ada_w, ada_b, norm1_g, ffn1_w1, ffn1_w3, ffn1_w2, norm2_g, w_in, dn_conv_w, dn_a_log, dn_dt_bias, dn_norm_g, mb_q_norm_g, mb_k_norm_g, rel_bias, w_proj_a, w_proj_b, w_out, norm3_g, ffn2_w1, ffn2_w3, ffn2_w2):
    b, t, d = x.shape
    m = b * t
    depth = ada_w.shape[0]
    wide = HEADS * HEAD_DIM
    cut0 = 4 * wide
    cut1 = cut0 + 2 * HEADS
    h = x.reshape(m, d)
    c_pad = jnp.zeros((8, d), F32).at[:b].set(c)
    for l in range(depth):
        ada = _ada(c_pad, ada_w[l], ada_b[l].reshape(1, -1))[:b]
        ada3 = ada.reshape(b, N_ADA, d)
        h = _ffn(h, ada3, norm1_g[l].reshape(1, d), ffn1_w1[l].astype(BF16), ffn1_w3[l].astype(BF16),
                 ffn1_w2[l].astype(BF16), slot=0, seq=t)

        w_main = jnp.concatenate([w_in[l][:, :cut0], w_in[l][:, cut1:]], axis=1).astype(BF16)
        w_ba = jnp.zeros((d, LANES), BF16).at[:, :2 * HEADS].set(w_in[l][:, cut0:cut1].astype(BF16))
        proj, ba = _inproj(h, ada3, norm2_g[l].reshape(1, d), w_main, w_ba, seq=t)
        proj3 = proj.reshape(b, t, -1)
        ba3 = ba.reshape(b, t, LANES)
        bat3 = jnp.transpose(ba3, (0, 2, 1))
        y_a = _deltanet(proj3, ba3, bat3, dn_conv_w[l], dn_a_log[l], dn_dt_bias[l], dn_norm_g[l])
        qn, kn, vt, kmean = _moba_prep(proj3, mb_q_norm_g[l], mb_k_norm_g[l])
        y_b = _moba(qn, kn, vt, kmean, rel_bias)
        merged = _merge(y_a.reshape(m, wide), y_b.reshape(m, wide), w_proj_a[l].astype(BF16),
                        w_proj_b[l].astype(BF16), proj, 7)
        gate2 = ada3[:, 5:6, :]
        h = _outproj(merged, w_out[l].astype(BF16), h, gate2, seq=t)

        h = _ffn(h, ada3, norm3_g[l].reshape(1, d), ffn2_w1[l].astype(BF16), ffn2_w3[l].astype(BF16),
                 ffn2_w2[l].astype(BF16), slot=6, seq=t)
    return h.reshape(b, t, d)
```

```python
import functools
import math

import jax
import jax.numpy as jnp
from jax import lax
from jax.experimental import pallas as pl
from jax.experimental.pallas import tpu as pltpu

F32 = jnp.float32
BF16 = jnp.bfloat16

LANES = 128
SUBLANES = 8
NORM_EPS = 1e-6
N_ADA = 9
HEADS = 16
HEAD_DIM = 128
DN_CONV = 4
DN_CHUNK = 128
DN_HEAD_BLOCK = 16
DN_INV_BASE = 8
MB_BLOCK = 256
MB_TOPK = 3
MB_HEAD_BLOCK = 8
MB_VROWS = HEAD_DIM + 16
RP_BUCKETS = 32
RP_MAX_DIST = 1024
NEG_BIG = -1e30
VMEM_LIMIT = 56 * 1024 * 1024

NT_DIMS = (((1,), (1,)), ((), ()))
NN_DIMS = (((1,), (0,)), ((), ()))
TN_DIMS = (((0,), (0,)), ((), ()))


def _dot(a, b, dims=NN_DIMS):
    return lax.dot_general(a.astype(BF16), b.astype(BF16), dims, preferred_element_type=F32)


def _split_bf16(x):
    hi = x.astype(BF16)
    lo = (x - hi.astype(F32)).astype(BF16)
    return hi, lo


def _dot3(a, b, dims=NN_DIMS):
    ah, al = _split_bf16(a)
    bh, bl = _split_bf16(b)
    f = lambda x, y: lax.dot_general(x, y, dims, preferred_element_type=F32)
    return f(ah, bh) + (f(ah, bl) + f(al, bh))


def _mm_split(a_split, b_split):
    ah, al = a_split
    bh, bl = b_split
    lhs = jnp.concatenate([ah, al], axis=1)
    top = jnp.concatenate([bh, bl], axis=1)
    r = jnp.dot(lhs, jnp.concatenate([top, top], axis=0), preferred_element_type=F32)
    n = bh.shape[1]
    return r[:, :n] + r[:, n:]


def _silu(x):
    return x * jax.nn.sigmoid(x)


def _params(*sem):
    return pltpu.CompilerParams(dimension_semantics=sem, vmem_limit_bytes=VMEM_LIMIT)


def _ada_kernel(c_ref, w_ref, b_ref, o_ref):
    s = _silu(c_ref[...])
    o_ref[...] = _dot3(s, w_ref[...]) + b_ref[...]


def _ada(c_pad, w, b):
    rows, d = c_pad.shape
    n = w.shape[1]
    tn = 1024
    return pl.pallas_call(
        _ada_kernel,
        out_shape=jax.ShapeDtypeStruct((rows, n), F32),
        grid=(n // tn,),
        in_specs=[pl.BlockSpec((rows, d), lambda j: (0, 0)),
                  pl.BlockSpec((d, tn), lambda j: (0, j)),
                  pl.BlockSpec((1, tn), lambda j: (0, j))],
        out_specs=pl.BlockSpec((rows, tn), lambda j: (0, j)),
        compiler_params=_params("arbitrary"),
        name="ada_proj",
    )(c_pad, w, b)


def _norm_modulate(h, gain, shift, scale):
    ms = jnp.mean(h * h, axis=-1, keepdims=True)
    y = h * lax.rsqrt(ms + NORM_EPS) * gain
    return y * (1.0 + scale) + shift


def _ffn_kernel(h_ref, ada_ref, g_ref, w1_ref, w3_ref, w2_ref, o_ref, u_ref, acc_ref, *, slot, nf):
    f = pl.program_id(1)

    @pl.when(f == 0)
    def _():
        u = _norm_modulate(h_ref[...], g_ref[...], ada_ref[0, slot:slot + 1, :], ada_ref[0, slot + 1:slot + 2, :])
        u_ref[...] = u.astype(BF16)
        acc_ref[...] = jnp.zeros_like(acc_ref)

    u = u_ref[...]
    a = jnp.dot(u, w1_ref[...], preferred_element_type=F32)
    b = jnp.dot(u, w3_ref[...], preferred_element_type=F32)
    acc_ref[...] += jnp.dot((_silu(a) * b).astype(BF16), w2_ref[...], preferred_element_type=F32)

    @pl.when(f == nf - 1)
    def _():
        o_ref[...] = h_ref[...] + 0.5 * ada_ref[0, slot + 2:slot + 3, :] * acc_ref[...]


def _ffn(h, ada3, gain, w1, w3, w2, *, slot, seq):
    m, d = h.shape
    dff = w1.shape[1]
    tm, tf = 512, 512
    nf = dff // tf
    per_batch = seq // tm
    return pl.pallas_call(
        functools.partial(_ffn_kernel, slot=slot, nf=nf),
        out_shape=jax.ShapeDtypeStruct((m, d), F32),
        grid=(m // tm, nf),
        in_specs=[pl.BlockSpec((tm, d), lambda i, f: (i, 0)),
                  pl.BlockSpec((1, N_ADA, d), lambda i, f: (i // per_batch, 0, 0)),
                  pl.BlockSpec((1, d), lambda i, f: (0, 0)),
                  pl.BlockSpec((d, tf), lambda i, f: (0, f)),
                  pl.BlockSpec((d, tf), lambda i, f: (0, f)),
                  pl.BlockSpec((tf, d), lambda i, f: (f, 0))],
        out_specs=pl.BlockSpec((tm, d), lambda i, f: (i, 0)),
        scratch_shapes=[pltpu.VMEM((tm, d), BF16), pltpu.VMEM((tm, d), F32)],
        compiler_params=_params("parallel", "arbitrary"),
        name="ffn",
    )(h, ada3, gain, w1, w3, w2)


def _inproj_kernel(h_ref, ada_ref, g_ref, w_ref, wba_ref, o_ref, ba_ref, u_ref):
    j = pl.program_id(1)

    @pl.when(j == 0)
    def _():
        u = _norm_modulate(h_ref[...], g_ref[...], ada_ref[0, 3:4, :], ada_ref[0, 4:5, :])
        u_ref[...] = u.astype(BF16)
        ba_ref[...] = jnp.dot(u_ref[...], wba_ref[...], preferred_element_type=F32)

    o_ref[...] = jnp.dot(u_ref[...], w_ref[...], preferred_element_type=F32).astype(o_ref.dtype)


def _inproj(h, ada3, gain, w, wba, *, seq):
    m, d = h.shape
    n = w.shape[1]
    tm, tn = 1024, 1024
    per_batch = seq // tm
    return pl.pallas_call(
        _inproj_kernel,
        out_shape=(jax.ShapeDtypeStruct((m, n), BF16), jax.ShapeDtypeStruct((m, LANES), F32)),
        grid=(m // tm, n // tn),
        in_specs=[pl.BlockSpec((tm, d), lambda i, j: (i, 0)),
                  pl.BlockSpec((1, N_ADA, d), lambda i, j: (i // per_batch, 0, 0)),
                  pl.BlockSpec((1, d), lambda i, j: (0, 0)),
                  pl.BlockSpec((d, tn), lambda i, j: (0, j)),
                  pl.BlockSpec((d, LANES), lambda i, j: (0, 0))],
        out_specs=(pl.BlockSpec((tm, tn), lambda i, j: (i, j)),
                   pl.BlockSpec((tm, LANES), lambda i, j: (i, 0))),
        scratch_shapes=[pltpu.VMEM((tm, d), BF16)],
        compiler_params=_params("parallel", "arbitrary"),
        name="in_proj",
    )(h, ada3, gain, w, wba)


def _softplus(x):
    return jnp.maximum(x, 0.0) + jnp.log1p(jnp.exp(-jnp.abs(x)))


def _dn_kernel(q_ref, k_ref, v_ref, z_ref, ba_ref, bat_ref, wq_ref, wk_ref, wv_ref,
               arow_ref, dtrow_ref, acol_ref, dtcol_ref, gain_ref, o_ref,
               prev_ref, state_ref, gcrow_ref, *, hb):
    c = DN_CHUNK
    hg = pl.program_id(1)
    n = pl.program_id(2)

    @pl.when(n == 0)
    def _():
        prev_ref[...] = jnp.zeros_like(prev_ref)
        state_ref[...] = jnp.zeros_like(state_ref)

    width = hb * HEAD_DIM
    sub = prev_ref.shape[1]
    row_t = lax.broadcasted_iota(jnp.int32, (sub, width), 0)

    def conv_silu(x_ref, w_ref, s):
        cur = x_ref[0].astype(F32)
        tail = prev_ref[s]
        acc = cur * w_ref[DN_CONV - 1:DN_CONV, :]
        for d in range(1, DN_CONV):
            shifted = pltpu.roll(cur, d, 0)
            head = jnp.where(row_t < d, pltpu.roll(tail, d, 0), shifted[:sub])
            shifted = jnp.concatenate([head, shifted[sub:]], axis=0)
            acc = acc + shifted * w_ref[DN_CONV - 1 - d:DN_CONV - d, :]
        prev_ref[s] = cur[c - sub:]
        return _silu(acc)

    xq = conv_silu(q_ref, wq_ref, 0)
    xk = conv_silu(k_ref, wk_ref, 1)
    xv = conv_silu(v_ref, wv_ref, 2)

    slab = ba_ref[0]
    beta_mat = jax.nn.sigmoid(slab)
    g_mat = -jnp.exp(arow_ref[...]) * _softplus(slab + dtrow_ref[...])
    ri = lax.broadcasted_iota(jnp.int32, (c, c), 0)
    ci = lax.broadcasted_iota(jnp.int32, (c, c), 1)
    lower_incl = (ri >= ci)
    tri_l = jnp.where(lower_incl, 1.0, 0.0).astype(BF16)
    tri_u = jnp.where(ri <= ci, 1.0, 0.0).astype(BF16)

    def exact_ones_dot(x, ones, left):
        x1 = x.astype(BF16)
        r1 = x - x1.astype(F32)
        x2 = r1.astype(BF16)
        x3 = (r1 - x2.astype(F32)).astype(BF16)
        if left:
            f = lambda p: jnp.dot(ones, p, preferred_element_type=F32)
        else:
            f = lambda p: jnp.dot(p, ones, preferred_element_type=F32)
        return f(x1) + (f(x2) + f(x3))

    gc_colmat = exact_ones_dot(g_mat, tri_l, True)
    a_rows = bat_ref[0, HEADS:2 * HEADS, :]
    g_rows = -jnp.exp(acol_ref[...]) * _softplus(a_rows + dtcol_ref[...])
    gcrow_ref[...] = exact_ones_dot(g_rows, tri_u, False)

    lane = lax.broadcasted_iota(jnp.int32, (c, LANES), 1)
    eye = jnp.where(ri == ci, 1.0, 0.0)
    strict = ri > ci
    scale = HEAD_DIM ** -0.5

    hs = []
    for hh in range(hb):
        h = hg * hb + hh
        sl = slice(hh * HEAD_DIM, (hh + 1) * HEAD_DIM)
        beta = jnp.sum(jnp.where(lane == h, beta_mat, 0.0), axis=1, keepdims=True)
        gcc = jnp.sum(jnp.where(lane == HEADS + h, gc_colmat, 0.0), axis=1, keepdims=True)
        gcr = gcrow_ref[pl.ds(h, 1), :]
        gc_last = gcr[:, c - 1:c]
        decay = jnp.where(lower_incl, jnp.exp(jnp.minimum(gcc - gcr, 0.0)), 0.0)
        q = xq[:, sl]
        k = xk[:, sl]
        q = q * (lax.rsqrt(jnp.sum(q * q, axis=-1, keepdims=True) + NORM_EPS) * scale)
        k = k * lax.rsqrt(jnp.sum(k * k, axis=-1, keepdims=True) + NORM_EPS)
        kb = k * beta
        egc = jnp.exp(gcc)
        hs.append(dict(hh=hh, sl=sl, decay=decay, q=q, k=k, kb=kb,
                       rhs=jnp.concatenate([xv[:, sl] * beta, kb * egc], axis=1),
                       qd=q * egc, kd=k * jnp.exp(gc_last - gcc), g_tot=jnp.exp(gc_last)))

    for s in hs:
        s["lmat"] = jnp.where(strict, _dot(s["kb"], s["k"], NT_DIMS) * s["decay"], 0.0)
        s["qk"] = _dot(s["q"], s["k"], NT_DIMS) * s["decay"]
    shift0 = int(math.log2(DN_INV_BASE))
    same_block = jnp.where((ri >> shift0) == (ci >> shift0), 1.0, 0.0).astype(BF16)
    for s in hs:
        lh, ll = _split_bf16(s["lmat"])
        s["l_sp"] = (lh, ll)
        s["pw_sp"] = (-(lh * same_block), -(ll * same_block))
        s["tmat"] = eye + s["pw_sp"][0].astype(F32) + s["pw_sp"][1].astype(F32)
    for _ in range(shift0 - 1):
        for s in hs:
            s["pw_sp"] = _split_bf16(_mm_split(s["pw_sp"], s["pw_sp"]))
        for s in hs:
            s["tmat"] = s["tmat"] + _mm_split(_split_bf16(s["tmat"]), s["pw_sp"])
    for sh in range(shift0, int(math.log2(c))):
        lower_left = jnp.where(jnp.logical_and((ri >> sh) == (ci >> sh) + 1, ((ci >> sh) & 1) == 0),
                               1.0, 0.0).astype(BF16)
        for s in hs:
            s["t_sp"] = _split_bf16(s["tmat"])
            lh, ll = s["l_sp"]
            s["pw_sp"] = _split_bf16(_mm_split((lh * lower_left, ll * lower_left), s["t_sp"]))
        for s in hs:
            s["tmat"] = s["tmat"] - _mm_split(s["t_sp"], s["pw_sp"])
    for s in hs:
        s["uw"] = _dot(s["tmat"], s["rhs"])
    for s in hs:
        s["state"] = state_ref[s["hh"]]
        s["v_new"] = s["uw"][:, :HEAD_DIM] - _dot(s["uw"][:, HEAD_DIM:], s["state"])
    for s in hs:
        s["o"] = _dot(s["qd"], s["state"]) + _dot(s["qk"], s["v_new"])
        state_ref[s["hh"]] = s["state"] * s["g_tot"] + _dot(s["kd"], s["v_new"], TN_DIMS)
    for s in hs:
        o = s["o"]
        o = o * lax.rsqrt(jnp.mean(o * o, axis=-1, keepdims=True) + NORM_EPS) * gain_ref[...]
        o_ref[0, :, s["sl"]] = (o * _silu(z_ref[0, :, s["sl"]].astype(F32))).astype(o_ref.dtype)


def _deltanet(proj3, ba3, bat3, conv_w, a_log, dt_bias, gain):
    b, t, _ = proj3.shape
    hb = DN_HEAD_BLOCK
    c = DN_CHUNK
    width = hb * HEAD_DIM
    ng = HEADS // hb
    region = (HEADS * HEAD_DIM) // width

    def pad_row(x):
        return jnp.zeros((1, LANES), F32).at[0, HEADS:2 * HEADS].set(x.astype(F32))

    arow, dtrow = pad_row(a_log), pad_row(dt_bias)
    acol = a_log.astype(F32).reshape(HEADS, 1)
    dtcol = dt_bias.astype(F32).reshape(HEADS, 1)
    gain2 = gain.astype(F32).reshape(1, HEAD_DIM)

    def col(r):
        return pl.BlockSpec((1, c, width), lambda bi, g, n, r=r: (bi, n, r * region + g))

    def wcol(r):
        return pl.BlockSpec((DN_CONV, width), lambda bi, g, n, r=r: (0, r * region + g))

    const = lambda shape: pl.BlockSpec(shape, lambda bi, g, n: tuple(0 for _ in shape))
    return pl.pallas_call(
        functools.partial(_dn_kernel, hb=hb),
        out_shape=jax.ShapeDtypeStruct((b, t, HEADS * HEAD_DIM), BF16),
        grid=(b, ng, t // c),
        in_specs=[col(0), col(1), col(2), col(3),
                  pl.BlockSpec((1, c, LANES), lambda bi, g, n: (bi, n, 0)),
                  pl.BlockSpec((1, LANES, c), lambda bi, g, n: (bi, 0, n)),
                  wcol(0), wcol(1), wcol(2),
                  const((1, LANES)), const((1, LANES)), const((HEADS, 1)), const((HEADS, 1)),
                  const((1, HEAD_DIM))],
        out_specs=pl.BlockSpec((1, c, width), lambda bi, g, n: (bi, n, g)),
        scratch_shapes=[pltpu.VMEM((3, SUBLANES, width), F32),
                        pltpu.VMEM((hb, HEAD_DIM, HEAD_DIM), F32),
                        pltpu.VMEM((HEADS, c), F32)],
        compiler_params=_params("parallel", "parallel", "arbitrary"),
        name="deltanet",
    )(proj3, proj3, proj3, proj3, ba3, bat3, conv_w, conv_w, conv_w, arow, dtrow, acol, dtcol, gain2)


def _moba_prep_kernel(q_ref, k_ref, v_ref, qg_ref, kg_ref, qn_ref, kn_ref, vt_ref, km_ref):
    n = pl.program_id(1)
    scale = HEAD_DIM ** -0.5

    @pl.when(n == 0)
    def _():
        km_ref[...] = jnp.zeros_like(km_ref)

    row = lax.broadcasted_iota(jnp.int32, (km_ref.shape[1], HEAD_DIM), 0)
    ones = jnp.ones((MB_VROWS - HEAD_DIM, MB_BLOCK), BF16)
    for h in range(HEADS):
        sl = slice(h * HEAD_DIM, (h + 1) * HEAD_DIM)
        q = q_ref[0, :, sl].astype(F32)
        k = k_ref[0, :, sl].astype(F32)
        qn = q * lax.rsqrt(jnp.mean(q * q, axis=-1, keepdims=True) + NORM_EPS) * qg_ref[...]
        kn = k * lax.rsqrt(jnp.mean(k * k, axis=-1, keepdims=True) + NORM_EPS) * kg_ref[...]
        qn_ref[0, :, sl] = qn * scale
        kn_ref[0, 0, :, sl] = kn.astype(BF16)
        km_ref[0, :, sl] = jnp.where(row == n, jnp.mean(kn, axis=0, keepdims=True), km_ref[0, :, sl])
        vt_ref[0, 0, h, :HEAD_DIM, :] = v_ref[0, :, sl].astype(F32).T.astype(BF16)
        vt_ref[0, 0, h, HEAD_DIM:, :] = ones


def _moba_prep(proj3, q_gain, k_gain):
    b, t, _ = proj3.shape
    nb = t // MB_BLOCK
    w = HEADS * HEAD_DIM
    col = lambda r: pl.BlockSpec((1, MB_BLOCK, w), lambda bi, n, r=r: (bi, n, r))
    gspec = pl.BlockSpec((1, HEAD_DIM), lambda bi, n: (0, 0))
    return pl.pallas_call(
        _moba_prep_kernel,
        out_shape=(jax.ShapeDtypeStruct((b, t, w), F32),
                   jax.ShapeDtypeStruct((b, nb, MB_BLOCK, w), BF16),
                   jax.ShapeDtypeStruct((b, nb, HEADS, MB_VROWS, MB_BLOCK), BF16),
                   jax.ShapeDtypeStruct((b, nb, w), F32)),
        grid=(b, nb),
        in_specs=[col(4), col(5), col(6), gspec, gspec],
        out_specs=(pl.BlockSpec((1, MB_BLOCK, w), lambda bi, n: (bi, n, 0)),
                   pl.BlockSpec((1, 1, MB_BLOCK, w), lambda bi, n: (bi, n, 0, 0)),
                   pl.BlockSpec((1, 1, HEADS, MB_VROWS, MB_BLOCK), lambda bi, n: (bi, n, 0, 0, 0)),
                   pl.BlockSpec((1, nb, w), lambda bi, n: (bi, 0, 0))),
        compiler_params=_params("parallel", "arbitrary"),
        name="moba_prep",
    )(proj3, proj3, proj3, q_gain.astype(F32).reshape(1, HEAD_DIM), k_gain.astype(F32).reshape(1, HEAD_DIM))


def _bucket_thresholds():
    max_exact = RP_BUCKETS // 2
    th = list(range(1, max_exact + 1))
    for kk in range(1, RP_BUCKETS - max_exact):
        x = max_exact * (RP_MAX_DIST / max_exact) ** (kk / (RP_BUCKETS - max_exact))
        r = round(x)
        th.append(r if abs(x - r) < 1e-9 else math.ceil(x))
    return th


N_NEAR = 5
LOG2E = 1.4426950408889634


def _moba_kernel(rb_ref, q_ref, k_ref, vt_ref, km_ref, o_ref, tab_ref, qa_ref, m_ref, alpha_ref, acc_ref, p_ref,
                 *, nb, hb):
    hg = pl.program_id(0)
    bi = pl.program_id(1)
    i = pl.program_id(2)
    blk = MB_BLOCK
    thresholds = _bucket_thresholds()
    heads = [(hh, slice(hh * HEAD_DIM, (hh + 1) * HEAD_DIM)) for hh in range(hb)]

    @pl.when(jnp.logical_and(bi == 0, i == 0))
    def _():
        ko = lax.broadcasted_iota(jnp.int32, (blk, blk), 0)
        qo = lax.broadcasted_iota(jnp.int32, (blk, blk), 1)
        for hh, _ in heads:
            h = hg * hb + hh
            far = rb_ref[RP_BUCKETS - 1, h]
            for delta in range(N_NEAR):
                d = delta * blk + qo - ko
                acc = jnp.full((blk, blk), rb_ref[0, h], F32)
                for bkt in range(1, RP_BUCKETS):
                    if delta * blk + blk - 1 >= thresholds[bkt - 1]:
                        acc = jnp.where(d >= thresholds[bkt - 1], rb_ref[bkt, h], acc)
                acc = (acc - far) * LOG2E
                if delta == 0:
                    acc = jnp.where(d >= 0, acc, NEG_BIG)
                tab_ref[hh, delta] = acc

    bidx = lax.broadcasted_iota(jnp.int32, (nb, blk), 0)
    past = bidx < i
    gates = []
    for hh, sl in heads:
        gate = _dot3(km_ref[0, :, sl], q_ref[0, :, sl], NT_DIMS)
        gates.append(jnp.where(past, gate, -jnp.inf))
    for (hh, sl), gate in zip(heads, gates):
        rank = jnp.zeros((nb, blk), jnp.int32)
        for jp in range(nb):
            gj = gate[jp:jp + 1, :]
            ahead = jnp.logical_or(gj > gate, jnp.logical_and(gj == gate, jp < bidx))
            rank = rank + ahead.astype(jnp.int32)
        allowed = jnp.logical_or(jnp.logical_and(past, rank < MB_TOPK), bidx == i)
        mask = jnp.where(allowed, 0.0, NEG_BIG)
        mask_t = jnp.concatenate([mask, jnp.zeros((LANES - nb, blk), F32)], axis=0).T
        qa_ref[hh] = jnp.concatenate([(q_ref[0, :, sl] * LOG2E).astype(BF16), mask_t.astype(BF16)], axis=1)

    m_ref[...] = jnp.full(m_ref.shape, NEG_BIG, F32)
    acc_ref[...] = jnp.zeros(acc_ref.shape, F32)
    lane = lax.broadcasted_iota(jnp.int32, (blk, LANES), 1)

    def logits(t, near):
        j = i - t
        onehot = jnp.where(lane == j, 1.0, 0.0).astype(BF16)
        ss = [lax.dot_general(jnp.concatenate([k_ref[0, j, :, sl], onehot], axis=1), qa_ref[hh], NT_DIMS,
                              preferred_element_type=F32) for hh, sl in heads]
        if near:
            ss = [s + tab_ref[hh, t] for (hh, _), s in zip(heads, ss)]
        return ss

    def values(t):
        slot = t & 1
        return [jnp.dot(vt_ref[0, i - t, hh], p_ref[slot, hh], preferred_element_type=F32) for hh, _ in heads]

    def probabilities(t, ss):
        slot = t & 1
        alphas = []
        for (hh, _), s in zip(heads, ss):
            m_old = m_ref[hh]
            m_new = jnp.maximum(m_old, jnp.max(s, axis=0, keepdims=True))
            m_ref[hh] = m_new
            p_ref[slot, hh] = jnp.exp2(s - m_new).astype(BF16)
            alphas.append(jnp.exp2(m_old - m_new))
        return alphas

    def accumulate(pvs):
        for (hh, _), pv in zip(heads, pvs):
            acc_ref[hh] = alpha_ref[hh] * acc_ref[hh] + pv

    def set_alpha(alphas):
        for (hh, _), alpha in zip(heads, alphas):
            alpha_ref[hh] = alpha

    def make_body(near):
        def body(t, carry):
            ss = logits(t, near)
            pvs = values(t - 1)
            alphas = probabilities(t, ss)
            accumulate(pvs)
            set_alpha(alphas)
            return carry
        return body

    set_alpha(probabilities(0, logits(0, True)))
    n_near = jnp.minimum(i + 1, N_NEAR)
    lax.fori_loop(1, n_near, make_body(True), 0)
    lax.fori_loop(n_near, i + 1, make_body(False), 0)
    accumulate(values(i))
    for hh, sl in heads:
        acc = acc_ref[hh]
        o_ref[0, :, sl] = (acc[:HEAD_DIM] / acc[HEAD_DIM:HEAD_DIM + 1]).T.astype(o_ref.dtype)


def _moba(qn, kn, vt, kmean, rel_bias):
    b, t, w = qn.shape
    nb = t // MB_BLOCK
    hb = MB_HEAD_BLOCK
    hw = hb * HEAD_DIM
    return pl.pallas_call(
        functools.partial(_moba_kernel, nb=nb, hb=hb),
        out_shape=jax.ShapeDtypeStruct((b, t, w), BF16),
        grid=(HEADS // hb, b, nb),
        in_specs=[pl.BlockSpec(memory_space=pltpu.SMEM),
                  pl.BlockSpec((1, MB_BLOCK, hw), lambda g, bi, i: (bi, i, g)),
                  pl.BlockSpec((1, nb, MB_BLOCK, hw), lambda g, bi, i: (bi, 0, 0, g), pipeline_mode=pl.Buffered(1)),
                  pl.BlockSpec((1, nb, hb, MB_VROWS, MB_BLOCK), lambda g, bi, i: (bi, 0, g, 0, 0),
                               pipeline_mode=pl.Buffered(1)),
                  pl.BlockSpec((1, nb, hw), lambda g, bi, i: (bi, 0, g))],
        out_specs=pl.BlockSpec((1, MB_BLOCK, hw), lambda g, bi, i: (bi, i, g)),
        scratch_shapes=[pltpu.VMEM((hb, N_NEAR, MB_BLOCK, MB_BLOCK), F32),
                        pltpu.VMEM((hb, MB_BLOCK, 2 * HEAD_DIM), BF16),
                        pltpu.VMEM((hb, 1, MB_BLOCK), F32),
                        pltpu.VMEM((hb, 1, MB_BLOCK), F32),
                        pltpu.VMEM((hb, MB_VROWS, MB_BLOCK), F32),
                        pltpu.VMEM((2, hb, MB_BLOCK, MB_BLOCK), BF16)],
        compiler_params=_params("arbitrary", "arbitrary", "arbitrary"),
        name="moba_attn",
    )(rel_bias.astype(F32), qn, kn, vt, kmean)


def _merge_kernel(ya_ref, yb_ref, wa_ref, wb_ref, ga_ref, gb_ref, o_ref):
    pa = jnp.dot(ya_ref[...], wa_ref[...], preferred_element_type=F32)
    pb = jnp.dot(yb_ref[...], wb_ref[...], preferred_element_type=F32)
    ga = ga_ref[...].astype(F32)
    gb = gb_ref[...].astype(F32)
    o_ref[...] = (jax.nn.sigmoid(ga) * pa + jax.nn.sigmoid(gb) * pb).astype(o_ref.dtype)


def _merge(ya, yb, wa, wb, proj, gate_col):
    m, d = ya.shape
    tm, tn = 1024, 512
    gpb = d // tn
    return pl.pallas_call(
        _merge_kernel,
        out_shape=jax.ShapeDtypeStruct((m, d), BF16),
        grid=(m // tm, d // tn),
        in_specs=[pl.BlockSpec((tm, d), lambda i, j: (i, 0)),
                  pl.BlockSpec((tm, d), lambda i, j: (i, 0)),
                  pl.BlockSpec((d, tn), lambda i, j: (0, j)),
                  pl.BlockSpec((d, tn), lambda i, j: (0, j)),
                  pl.BlockSpec((tm, tn), lambda i, j: (i, gate_col * gpb + j)),
                  pl.BlockSpec((tm, tn), lambda i, j: (i, (gate_col + 1) * gpb + j))],
        out_specs=pl.BlockSpec((tm, tn), lambda i, j: (i, j)),
        compiler_params=_params("parallel", "arbitrary"),
        name="merge",
    )(ya, yb, wa, wb, proj, proj)


def _outproj_kernel(x_ref, w_ref, h_ref, ada_ref, o_ref):
    o_ref[...] = h_ref[...] + ada_ref[0] * jnp.dot(x_ref[...], w_ref[...], preferred_element_type=F32)


def _outproj(x, w, h, gate3, *, seq):
    m, d = x.shape
    tm, tn = 1024, 512
    per_batch = seq // tm
    return pl.pallas_call(
        _outproj_kernel,
        out_shape=jax.ShapeDtypeStruct((m, d), F32),
        grid=(m // tm, d // tn),
        in_specs=[pl.BlockSpec((tm, d), lambda i, j: (i, 0)),
                  pl.BlockSpec((d, tn), lambda i, j: (0, j)),
                  pl.BlockSpec((tm, tn), lambda i, j: (i, j)),
                  pl.BlockSpec((1, 1, tn), lambda i, j: (i // per_batch, 0, j))],
        out_specs=pl.BlockSpec((tm, tn), lambda i, j: (i, j)),
        compiler_params=_params("parallel", "arbitrary"),
        name="out_proj",
    )(x, w, h, gate3)


def kernel(x, c, ada_w, ada_b, norm1_g, ffn1_w1, ffn1_w3, ffn1_w2, norm2_g, w_in, dn_conv_w, dn_a_log, dn_dt_bias, dn_norm_g, mb_q_norm_g, mb_k_norm_g, rel_bias, w_proj_a, w_proj_b, w_out, norm3_g, ffn2_w1, ffn2_w3, ffn2_w2):
    b, t, d = x.shape
    m = b * t
    depth = ada_w.shape[0]
    wide = HEADS * HEAD_DIM
    cut0 = 4 * wide
    cut1 = cut0 + 2 * HEADS
    h = x.reshape(m, d)
    c_pad = jnp.zeros((8, d), F32).at[:b].set(c)
    for l in range(depth):
        ada = _ada(c_pad, ada_w[l], ada_b[l].reshape(1, -1))[:b]
        ada3 = ada.reshape(b, N_ADA, d)
        h = _ffn(h, ada3, norm1_g[l].reshape(1, d), ffn1_w1[l].astype(BF16), ffn1_w3[l].astype(BF16),
                 ffn1_w2[l].astype(BF16), slot=0, seq=t)

        w_main = jnp.concatenate([w_in[l][:, :cut0], w_in[l][:, cut1:]], axis=1).astype(BF16)
        w_ba = jnp.zeros((d, LANES), BF16).at[:, :2 * HEADS].set(w_in[l][:, cut0:cut1].astype(BF16))
        proj, ba = _inproj(h, ada3, norm2_g[l].reshape(1, d), w_main, w_ba, seq=t)
        proj3 = proj.reshape(b, t, -1)
        ba3 = ba.reshape(b, t, LANES)
        bat3 = jnp.transpose(ba3, (0, 2, 1))
        y_a = _deltanet(proj3, ba3, bat3, dn_conv_w[l], dn_a_log[l], dn_dt_bias[l], dn_norm_g[l])
        qn, kn, vt, kmean = _moba_prep(proj3, mb_q_norm_g[l], mb_k_norm_g[l])
        y_b = _moba(qn, kn, vt, kmean, rel_bias)
        merged = _merge(y_a.reshape(m, wide), y_b.reshape(m, wide), w_proj_a[l].astype(BF16),
                        w_proj_b[l].astype(BF16), proj, 7)
        gate2 = ada3[:, 5:6, :]
        h = _outproj(merged, w_out[l].astype(BF16), h, gate2, seq=t)

        h = _ffn(h, ada3, norm3_g[l].reshape(1, d), ffn2_w1[l].astype(BF16), ffn2_w3[l].astype(BF16),
                 ffn2_w2[l].astype(BF16), slot=6, seq=t)
    return h.reshape(b, t, d)
```

```python
import functools
import math

import jax
import jax.numpy as jnp
from jax import lax
from jax.experimental import pallas as pl
from jax.experimental.pallas import tpu as pltpu

F32 = jnp.float32
BF16 = jnp.bfloat16

LANES = 128
SUBLANES = 8
NORM_EPS = 1e-6
N_ADA = 9
HEADS = 16
HEAD_DIM = 128
DN_CONV = 4
DN_CHUNK = 128
DN_HEAD_BLOCK = 16
DN_INV_BASE = 8
MB_BLOCK = 256
MB_TOPK = 3
MB_HEAD_BLOCK = 8
MB_QBLOCKS = 1
MB_VROWS = HEAD_DIM + 16
RP_BUCKETS = 32
RP_MAX_DIST = 1024
NEG_BIG = -1e30
VMEM_LIMIT = 56 * 1024 * 1024

NT_DIMS = (((1,), (1,)), ((), ()))
NN_DIMS = (((1,), (0,)), ((), ()))
TN_DIMS = (((0,), (0,)), ((), ()))


def _dot(a, b, dims=NN_DIMS):
    return lax.dot_general(a.astype(BF16), b.astype(BF16), dims, preferred_element_type=F32)


def _split_bf16(x):
    hi = x.astype(BF16)
    lo = (x - hi.astype(F32)).astype(BF16)
    return hi, lo


def _dot3(a, b, dims=NN_DIMS):
    ah, al = _split_bf16(a)
    bh, bl = _split_bf16(b)
    f = lambda x, y: lax.dot_general(x, y, dims, preferred_element_type=F32)
    return f(ah, bh) + (f(ah, bl) + f(al, bh))


def _silu(x):
    return x * jax.nn.sigmoid(x)


def _params(*sem):
    return pltpu.CompilerParams(dimension_semantics=sem, vmem_limit_bytes=VMEM_LIMIT)


def _ada_kernel(c_ref, w_ref, b_ref, o_ref):
    s = _silu(c_ref[...])
    o_ref[...] = _dot3(s, w_ref[0]) + b_ref[0]


def _ada(c_pad, w, b, layer):
    rows, d = c_pad.shape
    n = w.shape[2]
    tn = 1024
    return pl.pallas_call(
        _ada_kernel,
        out_shape=jax.ShapeDtypeStruct((rows, n), F32),
        grid=(n // tn,),
        in_specs=[pl.BlockSpec((rows, d), lambda j: (0, 0)),
                  pl.BlockSpec((1, d, tn), lambda j: (layer, 0, j)),
                  pl.BlockSpec((1, 1, tn), lambda j: (layer, 0, j))],
        out_specs=pl.BlockSpec((rows, tn), lambda j: (0, j)),
        compiler_params=_params("arbitrary"),
        name="ada_proj",
    )(c_pad, w, b)


def _norm_modulate(h, gain, shift, scale):
    ms = jnp.mean(h * h, axis=-1, keepdims=True)
    return (h * lax.rsqrt(ms + NORM_EPS)) * (gain * (1.0 + scale)) + shift


def _ffn_kernel(h_ref, ada_ref, g_ref, w1_ref, w3_ref, w2_ref, o_ref, u_ref, acc_ref, *, slot, nf):
    f = pl.program_id(1)

    @pl.when(f == 0)
    def _():
        u = _norm_modulate(h_ref[...], g_ref[...], ada_ref[0, slot:slot + 1, :], ada_ref[0, slot + 1:slot + 2, :])
        u_ref[...] = u.astype(BF16)
        acc_ref[...] = jnp.zeros_like(acc_ref)

    u = u_ref[...]
    a = jnp.dot(u, w1_ref[...], preferred_element_type=F32)
    b = jnp.dot(u, w3_ref[...], preferred_element_type=F32)
    acc_ref[...] += jnp.dot((_silu(a) * b).astype(BF16), w2_ref[...], preferred_element_type=F32)

    @pl.when(f == nf - 1)
    def _():
        o_ref[...] = h_ref[...] + 0.5 * ada_ref[0, slot + 2:slot + 3, :] * acc_ref[...]


def _ffn(h, ada3, gain, w1, w3, w2, *, slot, seq):
    m, d = h.shape
    dff = w1.shape[1]
    tm, tf = 512, 512
    nf = dff // tf
    per_batch = seq // tm
    return pl.pallas_call(
        functools.partial(_ffn_kernel, slot=slot, nf=nf),
        out_shape=jax.ShapeDtypeStruct((m, d), F32),
        grid=(m // tm, nf),
        in_specs=[pl.BlockSpec((tm, d), lambda i, f: (i, 0)),
                  pl.BlockSpec((1, N_ADA, d), lambda i, f: (i // per_batch, 0, 0)),
                  pl.BlockSpec((1, d), lambda i, f: (0, 0)),
                  pl.BlockSpec((d, tf), lambda i, f: (0, f)),
                  pl.BlockSpec((d, tf), lambda i, f: (0, f)),
                  pl.BlockSpec((tf, d), lambda i, f: (f, 0))],
        out_specs=pl.BlockSpec((tm, d), lambda i, f: (i, 0)),
        scratch_shapes=[pltpu.VMEM((tm, d), BF16), pltpu.VMEM((tm, d), F32)],
        compiler_params=_params("parallel", "arbitrary"),
        name="ffn",
    )(h, ada3, gain, w1, w3, w2)


def _inproj_kernel(h_ref, ada_ref, g_ref, w_ref, wba_ref, o_ref, ba_ref, u_ref):
    j = pl.program_id(1)

    @pl.when(j == 0)
    def _():
        u = _norm_modulate(h_ref[...], g_ref[...], ada_ref[0, 3:4, :], ada_ref[0, 4:5, :])
        u_ref[...] = u.astype(BF16)
        ba_ref[...] = jnp.dot(u_ref[...], wba_ref[...], preferred_element_type=F32)

    o_ref[...] = jnp.dot(u_ref[...], w_ref[...], preferred_element_type=F32).astype(o_ref.dtype)


def _inproj(h, ada3, gain, w, wba, *, seq):
    m, d = h.shape
    n = w.shape[1]
    tm, tn = 1024, 1024
    per_batch = seq // tm
    return pl.pallas_call(
        _inproj_kernel,
        out_shape=(jax.ShapeDtypeStruct((m, n), BF16), jax.ShapeDtypeStruct((m, LANES), F32)),
        grid=(m // tm, n // tn),
        in_specs=[pl.BlockSpec((tm, d), lambda i, j: (i, 0)),
                  pl.BlockSpec((1, N_ADA, d), lambda i, j: (i // per_batch, 0, 0)),
                  pl.BlockSpec((1, d), lambda i, j: (0, 0)),
                  pl.BlockSpec((d, tn), lambda i, j: (0, j)),
                  pl.BlockSpec((d, LANES), lambda i, j: (0, 0))],
        out_specs=(pl.BlockSpec((tm, tn), lambda i, j: (i, j)),
                   pl.BlockSpec((tm, LANES), lambda i, j: (i, 0))),
        scratch_shapes=[pltpu.VMEM((tm, d), BF16)],
        compiler_params=_params("parallel", "arbitrary"),
        name="in_proj",
    )(h, ada3, gain, w, wba)


def _softplus(x):
    return jnp.maximum(x, 0.0) + jnp.log1p(jnp.exp(-jnp.abs(x)))


def _dn_kernel(q_ref, k_ref, v_ref, z_ref, ba_ref, bat_ref, wq_ref, wk_ref, wv_ref,
               arow_ref, dtrow_ref, acol_ref, dtcol_ref, gain_ref, o_ref,
               prev_ref, state_ref, gcrow_ref, *, hb):
    c = DN_CHUNK
    hg = pl.program_id(1)
    n = pl.program_id(2)

    @pl.when(n == 0)
    def _():
        prev_ref[...] = jnp.zeros_like(prev_ref)
        state_ref[...] = jnp.zeros_like(state_ref)

    width = hb * HEAD_DIM
    sub = prev_ref.shape[1]
    row_t = lax.broadcasted_iota(jnp.int32, (sub, width), 0)

    def conv_silu(x_ref, w_ref, s):
        cur = x_ref[0].astype(F32)
        tail = prev_ref[s]
        acc = cur * w_ref[DN_CONV - 1:DN_CONV, :]
        for d in range(1, DN_CONV):
            shifted = pltpu.roll(cur, d, 0)
            head = jnp.where(row_t < d, pltpu.roll(tail, d, 0), shifted[:sub])
            shifted = jnp.concatenate([head, shifted[sub:]], axis=0)
            acc = acc + shifted * w_ref[DN_CONV - 1 - d:DN_CONV - d, :]
        prev_ref[s] = cur[c - sub:]
        return _silu(acc)

    xq = conv_silu(q_ref, wq_ref, 0)
    xk = conv_silu(k_ref, wk_ref, 1)
    xv = conv_silu(v_ref, wv_ref, 2)

    slab = ba_ref[0]
    beta_mat = jax.nn.sigmoid(slab)
    g_mat = -jnp.exp(arow_ref[...]) * _softplus(slab + dtrow_ref[...])
    ri = lax.broadcasted_iota(jnp.int32, (c, c), 0)
    ci = lax.broadcasted_iota(jnp.int32, (c, c), 1)
    lower_incl = (ri >= ci)
    tri_l = jnp.where(lower_incl, 1.0, 0.0).astype(BF16)
    tri_u = jnp.where(ri <= ci, 1.0, 0.0).astype(BF16)

    def exact_ones_dot(x, ones, left):
        x1 = x.astype(BF16)
        r1 = x - x1.astype(F32)
        x2 = r1.astype(BF16)
        x3 = (r1 - x2.astype(F32)).astype(BF16)
        if left:
            f = lambda p: jnp.dot(ones, p, preferred_element_type=F32)
        else:
            f = lambda p: jnp.dot(p, ones, preferred_element_type=F32)
        return f(x1) + (f(x2) + f(x3))

    gc_colmat = exact_ones_dot(g_mat, tri_l, True)
    a_rows = bat_ref[0, HEADS:2 * HEADS, :]
    g_rows = -jnp.exp(acol_ref[...]) * _softplus(a_rows + dtcol_ref[...])
    gcrow_ref[...] = exact_ones_dot(g_rows, tri_u, False)

    lane = lax.broadcasted_iota(jnp.int32, (c, LANES), 1)
    eye = jnp.where(ri == ci, 1.0, 0.0)
    strict = ri > ci
    scale = HEAD_DIM ** -0.5

    hs = []
    for hh in range(hb):
        h = hg * hb + hh
        sl = slice(hh * HEAD_DIM, (hh + 1) * HEAD_DIM)
        beta = jnp.sum(jnp.where(lane == h, beta_mat, 0.0), axis=1, keepdims=True)
        gcc = jnp.sum(jnp.where(lane == HEADS + h, gc_colmat, 0.0), axis=1, keepdims=True)
        gcr = gcrow_ref[pl.ds(h, 1), :]
        gc_last = gcr[:, c - 1:c]
        decay = jnp.where(lower_incl, jnp.exp(jnp.minimum(gcc - gcr, 0.0)), 0.0)
        q = xq[:, sl]
        k = xk[:, sl]
        q = q * (lax.rsqrt(jnp.sum(q * q, axis=-1, keepdims=True) + NORM_EPS) * scale)
        k = k * lax.rsqrt(jnp.sum(k * k, axis=-1, keepdims=True) + NORM_EPS)
        kb = k * beta
        egc = jnp.exp(gcc)
        hs.append(dict(hh=hh, sl=sl, decay=decay, q=q, k=k, kb=kb,
                       rhs=jnp.concatenate([xv[:, sl] * beta, kb * egc], axis=1),
                       qd=q * egc, kd=k * jnp.exp(gc_last - gcc), g_tot=jnp.exp(gc_last)))

    for s in hs:
        s["lmat"] = jnp.where(strict, _dot(s["kb"], s["k"], NT_DIMS) * s["decay"], 0.0)
        s["qk"] = _dot(s["q"], s["k"], NT_DIMS) * s["decay"]
    shift0 = int(math.log2(DN_INV_BASE))
    mm = lambda a, b: jnp.dot(a, b, preferred_element_type=F32)
    same_block = jnp.where((ri >> shift0) == (ci >> shift0), 1.0, 0.0).astype(BF16)
    for s in hs:
        s["lb"] = s["lmat"].astype(BF16)
        s["pw"] = -(s["lb"] * same_block)
        s["tmat"] = eye + s["pw"].astype(F32)
    for _ in range(shift0 - 1):
        for s in hs:
            s["pw"] = mm(s["pw"], s["pw"]).astype(BF16)
        for s in hs:
            s["tmat"] = s["tmat"] + mm(s["tmat"].astype(BF16), s["pw"])
    for sh in range(shift0, int(math.log2(c))):
        lower_left = jnp.where(jnp.logical_and((ri >> sh) == (ci >> sh) + 1, ((ci >> sh) & 1) == 0),
                               1.0, 0.0).astype(BF16)
        for s in hs:
            s["tb"] = s["tmat"].astype(BF16)
            s["pw"] = mm(s["lb"] * lower_left, s["tb"]).astype(BF16)
        for s in hs:
            s["tmat"] = s["tmat"] - mm(s["tb"], s["pw"])
    for s in hs:
        s["uw"] = _dot(s["tmat"], s["rhs"])
    for s in hs:
        s["state"] = state_ref[s["hh"]]
        s["v_new"] = s["uw"][:, :HEAD_DIM] - _dot(s["uw"][:, HEAD_DIM:], s["state"])
    for s in hs:
        s["o"] = _dot(s["qd"], s["state"]) + _dot(s["qk"], s["v_new"])
        state_ref[s["hh"]] = s["state"] * s["g_tot"] + _dot(s["kd"], s["v_new"], TN_DIMS)
    for s in hs:
        o = s["o"]
        o = o * lax.rsqrt(jnp.mean(o * o, axis=-1, keepdims=True) + NORM_EPS) * gain_ref[...]
        o_ref[0, :, s["sl"]] = (o * _silu(z_ref[0, :, s["sl"]].astype(F32))).astype(o_ref.dtype)


def _deltanet(proj3, ba3, bat3, conv_w, a_log, dt_bias, gain):
    b, t, _ = proj3.shape
    hb = DN_HEAD_BLOCK
    c = DN_CHUNK
    width = hb * HEAD_DIM
    ng = HEADS // hb
    region = (HEADS * HEAD_DIM) // width

    def pad_row(x):
        return jnp.zeros((1, LANES), F32).at[0, HEADS:2 * HEADS].set(x.astype(F32))

    arow, dtrow = pad_row(a_log), pad_row(dt_bias)
    acol = a_log.astype(F32).reshape(HEADS, 1)
    dtcol = dt_bias.astype(F32).reshape(HEADS, 1)
    gain2 = gain.astype(F32).reshape(1, HEAD_DIM)

    def col(r):
        return pl.BlockSpec((1, c, width), lambda bi, g, n, r=r: (bi, n, r * region + g))

    def wcol(r):
        return pl.BlockSpec((DN_CONV, width), lambda bi, g, n, r=r: (0, r * region + g))

    const = lambda shape: pl.BlockSpec(shape, lambda bi, g, n: tuple(0 for _ in shape))
    return pl.pallas_call(
        functools.partial(_dn_kernel, hb=hb),
        out_shape=jax.ShapeDtypeStruct((b, t, HEADS * HEAD_DIM), BF16),
        grid=(b, ng, t // c),
        in_specs=[col(0), col(1), col(2), col(3),
                  pl.BlockSpec((1, c, LANES), lambda bi, g, n: (bi, n, 0)),
                  pl.BlockSpec((1, LANES, c), lambda bi, g, n: (bi, 0, n)),
                  wcol(0), wcol(1), wcol(2),
                  const((1, LANES)), const((1, LANES)), const((HEADS, 1)), const((HEADS, 1)),
                  const((1, HEAD_DIM))],
        out_specs=pl.BlockSpec((1, c, width), lambda bi, g, n: (bi, n, g)),
        scratch_shapes=[pltpu.VMEM((3, SUBLANES, width), F32),
                        pltpu.VMEM((hb, HEAD_DIM, HEAD_DIM), F32),
                        pltpu.VMEM((HEADS, c), F32)],
        compiler_params=_params("parallel", "parallel", "arbitrary"),
        name="deltanet",
    )(proj3, proj3, proj3, proj3, ba3, bat3, conv_w, conv_w, conv_w, arow, dtrow, acol, dtcol, gain2)


def _moba_prep_kernel(q_ref, k_ref, v_ref, qg_ref, kg_ref, qn_ref, kn_ref, vt_ref, km_ref):
    n = pl.program_id(1)
    scale = HEAD_DIM ** -0.5

    @pl.when(n == 0)
    def _():
        km_ref[...] = jnp.zeros_like(km_ref)

    row = lax.broadcasted_iota(jnp.int32, (km_ref.shape[1], HEAD_DIM), 0)
    ones = jnp.ones((MB_VROWS - HEAD_DIM, MB_BLOCK), BF16)
    lane = lax.broadcasted_iota(jnp.int32, (MB_BLOCK, LANES), 1)
    onehot = jnp.where(lane == n, 1.0, 0.0).astype(BF16)
    for h in range(HEADS):
        sl = slice(h * HEAD_DIM, (h + 1) * HEAD_DIM)
        ka = 2 * h * HEAD_DIM
        q = q_ref[0, :, sl].astype(F32)
        k = k_ref[0, :, sl].astype(F32)
        qn = q * lax.rsqrt(jnp.mean(q * q, axis=-1, keepdims=True) + NORM_EPS) * qg_ref[...]
        kn = k * lax.rsqrt(jnp.mean(k * k, axis=-1, keepdims=True) + NORM_EPS) * kg_ref[...]
        qn_ref[0, :, sl] = (qn * scale).astype(qn_ref.dtype)
        kn_ref[0, 0, :, ka:ka + HEAD_DIM] = kn.astype(BF16)
        kn_ref[0, 0, :, ka + HEAD_DIM:ka + 2 * HEAD_DIM] = onehot
        km_ref[0, :, sl] = jnp.where(row == n, jnp.mean(kn, axis=0, keepdims=True), km_ref[0, :, sl])
        vt_ref[0, 0, h, :HEAD_DIM, :] = v_ref[0, :, sl].astype(F32).T.astype(BF16)
        vt_ref[0, 0, h, HEAD_DIM:, :] = ones


def _moba_prep(proj3, q_gain, k_gain):
    b, t, _ = proj3.shape
    nb = t // MB_BLOCK
    w = HEADS * HEAD_DIM
    col = lambda r: pl.BlockSpec((1, MB_BLOCK, w), lambda bi, n, r=r: (bi, n, r))
    gspec = pl.BlockSpec((1, HEAD_DIM), lambda bi, n: (0, 0))
    return pl.pallas_call(
        _moba_prep_kernel,
        out_shape=(jax.ShapeDtypeStruct((b, t, w), BF16),
                   jax.ShapeDtypeStruct((b, nb, MB_BLOCK, 2 * w), BF16),
                   jax.ShapeDtypeStruct((b, nb, HEADS, MB_VROWS, MB_BLOCK), BF16),
                   jax.ShapeDtypeStruct((b, nb, w), F32)),
        grid=(b, nb),
        in_specs=[col(4), col(5), col(6), gspec, gspec],
        out_specs=(pl.BlockSpec((1, MB_BLOCK, w), lambda bi, n: (bi, n, 0)),
                   pl.BlockSpec((1, 1, MB_BLOCK, 2 * w), lambda bi, n: (bi, n, 0, 0)),
                   pl.BlockSpec((1, 1, HEADS, MB_VROWS, MB_BLOCK), lambda bi, n: (bi, n, 0, 0, 0)),
                   pl.BlockSpec((1, nb, w), lambda bi, n: (bi, 0, 0))),
        compiler_params=_params("parallel", "arbitrary"),
        name="moba_prep",
    )(proj3, proj3, proj3, q_gain.astype(F32).reshape(1, HEAD_DIM), k_gain.astype(F32).reshape(1, HEAD_DIM))


def _bucket_thresholds():
    max_exact = RP_BUCKETS // 2
    th = list(range(1, max_exact + 1))
    for kk in range(1, RP_BUCKETS - max_exact):
        x = max_exact * (RP_MAX_DIST / max_exact) ** (kk / (RP_BUCKETS - max_exact))
        r = round(x)
        th.append(r if abs(x - r) < 1e-9 else math.ceil(x))
    return th


N_NEAR = 5
LOG2E = 1.4426950408889634


def _moba_kernel(rb_ref, q_ref, k_ref, vt_ref, km_ref, o_ref, tab_ref, qa_ref, m_ref, alpha_ref, acc_ref, p_ref,
                 *, nb, hb):
    hg = pl.program_id(0)
    bi = pl.program_id(1)
    it = pl.program_id(2)
    blk = MB_BLOCK
    nq = MB_QBLOCKS
    tq = nq * blk
    shift = int(math.log2(blk))
    last = nq * it + nq - 1
    thresholds = _bucket_thresholds()
    heads = list(range(hb))
    hsl = lambda hh: slice(hh * HEAD_DIM, (hh + 1) * HEAD_DIM)

    @pl.when(jnp.logical_and(bi == 0, it == 0))
    def _():
        ko = lax.broadcasted_iota(jnp.int32, (blk, blk), 0)
        qo = lax.broadcasted_iota(jnp.int32, (blk, blk), 1)
        for hh in heads:
            h = hg * hb + hh
            far = rb_ref[RP_BUCKETS - 1, h]
            for delta in range(N_NEAR):
                d = delta * blk + qo - ko
                acc = jnp.full((blk, blk), rb_ref[0, h], F32)
                for bkt in range(1, RP_BUCKETS):
                    if delta * blk + blk - 1 >= thresholds[bkt - 1]:
                        acc = jnp.where(d >= thresholds[bkt - 1], rb_ref[bkt, h], acc)
                acc = (acc - far) * LOG2E
                if delta == 0:
                    acc = jnp.where(d >= 0, acc, NEG_BIG)
                tab_ref[hh, delta] = acc
            tab_ref[hh, N_NEAR] = jnp.zeros((blk, blk), F32)

    bidx = lax.broadcasted_iota(jnp.int32, (nb, tq), 0)
    qblk = nq * it + (lax.broadcasted_iota(jnp.int32, (nb, tq), 1) >> shift)
    past = bidx < qblk
    gates = []
    for hh in heads:
        kmh, kml = _split_bf16(km_ref[0, :, hsl(hh)])
        qb = q_ref[0, :, hsl(hh)]
        gate = (lax.dot_general(kmh, qb, NT_DIMS, preferred_element_type=F32)
                + lax.dot_general(kml, qb, NT_DIMS, preferred_element_type=F32))
        gates.append(jnp.where(past, gate, -jnp.inf))
    for hh, gate in zip(heads, gates):
        rank = jnp.zeros((nb, tq), jnp.int32)
        for jp in range(nb):
            gj = gate[jp:jp + 1, :]
            ahead = jnp.logical_or(gj > gate, jnp.logical_and(gj == gate, jp < bidx))
            rank = rank + ahead.astype(jnp.int32)
        allowed = jnp.logical_or(jnp.logical_and(past, rank < MB_TOPK), bidx == qblk)
        mask = jnp.where(allowed, 0.0, NEG_BIG)
        q_t = (q_ref[0, :, hsl(hh)].astype(F32) * LOG2E).T.astype(BF16)
        mask_rows = jnp.concatenate([mask, jnp.zeros((LANES - nb, tq), F32)], axis=0).astype(BF16)
        qa_ref[hh] = jnp.concatenate([q_t, mask_rows], axis=0)

    m_ref[...] = jnp.full(m_ref.shape, NEG_BIG, F32)
    acc_ref[...] = jnp.zeros(acc_ref.shape, F32)

    def logits(t, near):
        j = last - t
        ss = [jnp.dot(k_ref[0, j, :, 2 * hh * HEAD_DIM:2 * (hh + 1) * HEAD_DIM], qa_ref[hh],
                      preferred_element_type=F32) for hh in heads]
        if near:
            idx = [jnp.clip(t - (nq - 1 - a), 0, N_NEAR) for a in range(nq)]
            ss = [jnp.concatenate([s[:, a * blk:(a + 1) * blk] + tab_ref[hh, idx[a]] for a in range(nq)], axis=1)
                  for hh, s in zip(heads, ss)]
        return ss

    def values(t):
        slot = t & 1
        return [jnp.dot(vt_ref[0, last - t, hh], p_ref[slot, hh], preferred_element_type=F32) for hh in heads]

    def probabilities(t, ss):
        slot = t & 1
        alphas = []
        for hh, s in zip(heads, ss):
            for q0 in range(0, tq, LANES):
                qs = slice(q0, q0 + LANES)
                m_old = m_ref[hh, :, qs]
                m_new = jnp.maximum(m_old, jnp.max(s[:, qs], axis=0, keepdims=True))
                m_ref[hh, :, qs] = m_new
                p_ref[slot, hh, :, qs] = jnp.exp2(s[:, qs] - m_new).astype(BF16)
                alphas.append((hh, qs, jnp.exp2(m_old - m_new)))
        return alphas

    def accumulate(pvs):
        for hh, pv in zip(heads, pvs):
            acc_ref[hh] = alpha_ref[hh] * acc_ref[hh] + pv

    def set_alpha(alphas):
        for hh, qs, alpha in alphas:
            alpha_ref[hh, :, qs] = alpha

    def make_body(near):
        def body(t, carry):
            ss = logits(t, near)
            pvs = values(t - 1)
            alphas = probabilities(t, ss)
            accumulate(pvs)
            set_alpha(alphas)
            return carry
        return body

    set_alpha(probabilities(0, logits(0, True)))
    n_near = jnp.minimum(last + 1, N_NEAR + nq - 1)
    lax.fori_loop(1, n_near, make_body(True), 0)
    lax.fori_loop(n_near, last + 1, make_body(False), 0)
    accumulate(values(last))
    for hh in heads:
        acc = acc_ref[hh]
        o_ref[0, :, hsl(hh)] = (acc[:HEAD_DIM] / acc[HEAD_DIM:HEAD_DIM + 1]).T.astype(o_ref.dtype)


def _moba(qn, kn, vt, kmean, rel_bias):
    b, t, w = qn.shape
    nb = t // MB_BLOCK
    hb = MB_HEAD_BLOCK
    hw = hb * HEAD_DIM
    tq = MB_QBLOCKS * MB_BLOCK
    return pl.pallas_call(
        functools.partial(_moba_kernel, nb=nb, hb=hb),
        out_shape=jax.ShapeDtypeStruct((b, t, w), BF16),
        grid=(HEADS // hb, b, t // tq),
        in_specs=[pl.BlockSpec(memory_space=pltpu.SMEM),
                  pl.BlockSpec((1, tq, hw), lambda g, bi, i: (bi, i, g)),
                  pl.BlockSpec((1, nb, MB_BLOCK, 2 * hw), lambda g, bi, i: (bi, 0, 0, g),
                               pipeline_mode=pl.Buffered(1)),
                  pl.BlockSpec((1, nb, hb, MB_VROWS, MB_BLOCK), lambda g, bi, i: (bi, 0, g, 0, 0),
                               pipeline_mode=pl.Buffered(1)),
                  pl.BlockSpec((1, nb, hw), lambda g, bi, i: (bi, 0, g))],
        out_specs=pl.BlockSpec((1, tq, hw), lambda g, bi, i: (bi, i, g)),
        scratch_shapes=[pltpu.VMEM((hb, N_NEAR + 1, MB_BLOCK, MB_BLOCK), F32),
                        pltpu.VMEM((hb, 2 * HEAD_DIM, tq), BF16),
                        pltpu.VMEM((hb, 1, tq), F32),
                        pltpu.VMEM((hb, 1, tq), F32),
                        pltpu.VMEM((hb, MB_VROWS, tq), F32),
                        pltpu.VMEM((2, hb, MB_BLOCK, tq), BF16)],
        compiler_params=_params("arbitrary", "arbitrary", "arbitrary"),
        name="moba_attn",
    )(rel_bias.astype(F32), qn, kn, vt, kmean)


def _merge_kernel(ya_ref, yb_ref, wa_ref, wb_ref, ga_ref, gb_ref, o_ref):
    pa = jnp.dot(ya_ref[...], wa_ref[...], preferred_element_type=F32)
    pb = jnp.dot(yb_ref[...], wb_ref[...], preferred_element_type=F32)
    ga = ga_ref[...].astype(F32)
    gb = gb_ref[...].astype(F32)
    o_ref[...] = (jax.nn.sigmoid(ga) * pa + jax.nn.sigmoid(gb) * pb).astype(o_ref.dtype)


def _merge(ya, yb, wa, wb, proj, gate_col):
    m, d = ya.shape
    tm, tn = 1024, 512
    gpb = d // tn
    return pl.pallas_call(
        _merge_kernel,
        out_shape=jax.ShapeDtypeStruct((m, d), BF16),
        grid=(m // tm, d // tn),
        in_specs=[pl.BlockSpec((tm, d), lambda i, j: (i, 0)),
                  pl.BlockSpec((tm, d), lambda i, j: (i, 0)),
                  pl.BlockSpec((d, tn), lambda i, j: (0, j)),
                  pl.BlockSpec((d, tn), lambda i, j: (0, j)),
                  pl.BlockSpec((tm, tn), lambda i, j: (i, gate_col * gpb + j)),
                  pl.BlockSpec((tm, tn), lambda i, j: (i, (gate_col + 1) * gpb + j))],
        out_specs=pl.BlockSpec((tm, tn), lambda i, j: (i, j)),
        compiler_params=_params("parallel", "arbitrary"),
        name="merge",
    )(ya, yb, wa, wb, proj, proj)


def _outproj_kernel(x_ref, w_ref, h_ref, ada_ref, o_ref):
    o_ref[...] = h_ref[...] + ada_ref[0] * jnp.dot(x_ref[...], w_ref[...], preferred_element_type=F32)


def _outproj(x, w, h, gate3, *, seq):
    m, d = x.shape
    tm, tn = 1024, 512
    per_batch = seq // tm
    return pl.pallas_call(
        _outproj_kernel,
        out_shape=jax.ShapeDtypeStruct((m, d), F32),
        grid=(m // tm, d // tn),
        in_specs=[pl.BlockSpec((tm, d), lambda i, j: (i, 0)),
                  pl.BlockSpec((d, tn), lambda i, j: (0, j)),
                  pl.BlockSpec((tm, tn), lambda i, j: (i, j)),
                  pl.BlockSpec((1, 1, tn), lambda i, j: (i // per_batch, 0, j))],
        out_specs=pl.BlockSpec((tm, tn), lambda i, j: (i, j)),
        compiler_params=_params("parallel", "arbitrary"),
        name="out_proj",
    )(x, w, h, gate3)


def kernel(x, c, ada_w, ada_b, norm1_g, ffn1_w1, ffn1_w3, ffn1_w2, norm2_g, w_in, dn_conv_w, dn_a_log, dn_dt_bias, dn_norm_g, mb_q_norm_g, mb_k_norm_g, rel_bias, w_proj_a, w_proj_b, w_out, norm3_g, ffn2_w1, ffn2_w3, ffn2_w2):
    b, t, d = x.shape
    m = b * t
    depth = ada_w.shape[0]
    wide = HEADS * HEAD_DIM
    cut0 = 4 * wide
    cut1 = cut0 + 2 * HEADS
    h = x.reshape(m, d)
    c_pad = jnp.zeros((8, d), F32).at[:b].set(c)
    for l in range(depth):
        ada = _ada(c_pad, ada_w, ada_b.reshape(depth, 1, -1), l)[:b]
        ada3 = ada.reshape(b, N_ADA, d)
        h = _ffn(h, ada3, norm1_g[l].reshape(1, d), ffn1_w1[l].astype(BF16), ffn1_w3[l].astype(BF16),
                 ffn1_w2[l].astype(BF16), slot=0, seq=t)

        w_main = jnp.concatenate([w_in[l][:, :cut0], w_in[l][:, cut1:]], axis=1).astype(BF16)
        w_ba = jnp.zeros((d, LANES), BF16).at[:, :2 * HEADS].set(w_in[l][:, cut0:cut1].astype(BF16))
        proj, ba = _inproj(h, ada3, norm2_g[l].reshape(1, d), w_main, w_ba, seq=t)
        proj3 = proj.reshape(b, t, -1)
        ba3 = ba.reshape(b, t, LANES)
        bat3 = jnp.transpose(ba3, (0, 2, 1))
        y_a = _deltanet(proj3, ba3, bat3, dn_conv_w[l], dn_a_log[l], dn_dt_bias[l], dn_norm_g[l])
        qn, kn, vt, kmean = _moba_prep(proj3, mb_q_norm_g[l], mb_k_norm_g[l])
        y_b = _moba(qn, kn, vt, kmean, rel_bias)
        merged = _merge(y_a.reshape(m, wide), y_b.reshape(m, wide), w_proj_a[l].astype(BF16),
                        w_proj_b[l].astype(BF16), proj, 7)
        gate2 = ada3[:, 5:6, :]
        h = _outproj(merged, w_out[l].astype(BF16), h, gate2, seq=t)

        h = _ffn(h, ada3, norm3_g[l].reshape(1, d), ffn2_w1[l].astype(BF16), ffn2_w3[l].astype(BF16),
                 ffn2_w2[l].astype(BF16), slot=6, seq=t)
    return h.reshape(b, t, d)
```

```python
import functools
import math

import jax
import jax.numpy as jnp
from jax import lax
from jax.experimental import pallas as pl
from jax.experimental.pallas import tpu as pltpu

F32 = jnp.float32
BF16 = jnp.bfloat16

LANES = 128
SUBLANES = 8
NORM_EPS = 1e-6
N_ADA = 9
HEADS = 16
HEAD_DIM = 128
DN_CONV = 4
DN_CHUNK = 128
DN_HEAD_BLOCK = 16
DN_INV_BASE = 8
MB_BLOCK = 256
MB_TOPK = 3
MB_HEAD_BLOCK = 8
MB_QBLOCKS = 1
MB_ISSUE_GROUP = 1
MB_ISSUE_AHEAD = 2
MB_VROWS = HEAD_DIM + 16
RP_BUCKETS = 32
RP_MAX_DIST = 1024
NEG_BIG = -1e30
VMEM_LIMIT = 56 * 1024 * 1024

NT_DIMS = (((1,), (1,)), ((), ()))
NN_DIMS = (((1,), (0,)), ((), ()))
TN_DIMS = (((0,), (0,)), ((), ()))


def _dot(a, b, dims=NN_DIMS):
    return lax.dot_general(a.astype(BF16), b.astype(BF16), dims, preferred_element_type=F32)


def _split_bf16(x):
    hi = x.astype(BF16)
    lo = (x - hi.astype(F32)).astype(BF16)
    return hi, lo


def _dot3(a, b, dims=NN_DIMS):
    ah, al = _split_bf16(a)
    bh, bl = _split_bf16(b)
    f = lambda x, y: lax.dot_general(x, y, dims, preferred_element_type=F32)
    return f(ah, bh) + (f(ah, bl) + f(al, bh))


def _silu(x):
    return x * jax.nn.sigmoid(x)


def _params(*sem):
    return pltpu.CompilerParams(dimension_semantics=sem, vmem_limit_bytes=VMEM_LIMIT)


def _ada_kernel(c_ref, w_ref, b_ref, o_ref):
    s = _silu(c_ref[...])
    o_ref[...] = _dot3(s, w_ref[0]) + b_ref[0]


def _ada(c_pad, w, b, layer):
    rows, d = c_pad.shape
    n = w.shape[2]
    tn = 1024
    return pl.pallas_call(
        _ada_kernel,
        out_shape=jax.ShapeDtypeStruct((rows, n), F32),
        grid=(n // tn,),
        in_specs=[pl.BlockSpec((rows, d), lambda j: (0, 0)),
                  pl.BlockSpec((1, d, tn), lambda j: (layer, 0, j)),
                  pl.BlockSpec((1, 1, tn), lambda j: (layer, 0, j))],
        out_specs=pl.BlockSpec((rows, tn), lambda j: (0, j)),
        compiler_params=_params("arbitrary"),
        name="ada_proj",
    )(c_pad, w, b)


def _norm_modulate(h, gain, shift, scale):
    ms = jnp.mean(h * h, axis=-1, keepdims=True)
    return (h * lax.rsqrt(ms + NORM_EPS)) * (gain * (1.0 + scale)) + shift


def _ffn_kernel(h_ref, ada_ref, g_ref, w1_ref, w3_ref, w2_ref, o_ref, u_ref, acc_ref, *, slot, nf):
    f = pl.program_id(1)

    @pl.when(f == 0)
    def _():
        u = _norm_modulate(h_ref[...], g_ref[...], ada_ref[0, slot:slot + 1, :], ada_ref[0, slot + 1:slot + 2, :])
        u_ref[...] = u.astype(BF16)
        acc_ref[...] = jnp.zeros_like(acc_ref)

    u = u_ref[...]
    a = jnp.dot(u, w1_ref[...], preferred_element_type=F32)
    b = jnp.dot(u, w3_ref[...], preferred_element_type=F32)
    acc_ref[...] += jnp.dot((_silu(a) * b).astype(BF16), w2_ref[...], preferred_element_type=F32)

    @pl.when(f == nf - 1)
    def _():
        o_ref[...] = h_ref[...] + 0.5 * ada_ref[0, slot + 2:slot + 3, :] * acc_ref[...]


def _ffn(h, ada3, gain, w1, w3, w2, *, slot, seq):
    m, d = h.shape
    dff = w1.shape[1]
    tm, tf = 512, 512
    nf = dff // tf
    per_batch = seq // tm
    return pl.pallas_call(
        functools.partial(_ffn_kernel, slot=slot, nf=nf),
        out_shape=jax.ShapeDtypeStruct((m, d), F32),
        grid=(m // tm, nf),
        in_specs=[pl.BlockSpec((tm, d), lambda i, f: (i, 0)),
                  pl.BlockSpec((1, N_ADA, d), lambda i, f: (i // per_batch, 0, 0)),
                  pl.BlockSpec((1, d), lambda i, f: (0, 0)),
                  pl.BlockSpec((d, tf), lambda i, f: (0, f)),
                  pl.BlockSpec((d, tf), lambda i, f: (0, f)),
                  pl.BlockSpec((tf, d), lambda i, f: (f, 0))],
        out_specs=pl.BlockSpec((tm, d), lambda i, f: (i, 0)),
        scratch_shapes=[pltpu.VMEM((tm, d), BF16), pltpu.VMEM((tm, d), F32)],
        compiler_params=_params("parallel", "arbitrary"),
        name="ffn",
    )(h, ada3, gain, w1, w3, w2)


def _inproj_kernel(h_ref, ada_ref, g_ref, wa_ref, wb_ref, wba_ref, o_ref, ba_ref, u_ref, *, na):
    j = pl.program_id(1)

    @pl.when(j == 0)
    def _():
        u = _norm_modulate(h_ref[...], g_ref[...], ada_ref[0, 3:4, :], ada_ref[0, 4:5, :])
        u_ref[...] = u.astype(BF16)
        ba_ref[...] = jnp.dot(u_ref[...], wba_ref[...], preferred_element_type=F32)

    @pl.when(j < na)
    def _():
        o_ref[...] = jnp.dot(u_ref[...], wa_ref[0], preferred_element_type=F32).astype(o_ref.dtype)

    @pl.when(j >= na)
    def _():
        o_ref[...] = jnp.dot(u_ref[...], wb_ref[...], preferred_element_type=F32).astype(o_ref.dtype)


def _inproj(h, ada3, gain, w_all, layer, n_lead, wb, wba, *, seq):
    m, d = h.shape
    tm, tn = 1024, 1024
    na = n_lead // tn
    n = n_lead + wb.shape[1]
    per_batch = seq // tm
    return pl.pallas_call(
        functools.partial(_inproj_kernel, na=na),
        out_shape=(jax.ShapeDtypeStruct((m, n), BF16), jax.ShapeDtypeStruct((m, LANES), F32)),
        grid=(m // tm, n // tn),
        in_specs=[pl.BlockSpec((tm, d), lambda i, j: (i, 0)),
                  pl.BlockSpec((1, N_ADA, d), lambda i, j: (i // per_batch, 0, 0)),
                  pl.BlockSpec((1, d), lambda i, j: (0, 0)),
                  pl.BlockSpec((1, d, tn), lambda i, j: (layer, 0, jnp.minimum(j, na - 1))),
                  pl.BlockSpec((d, tn), lambda i, j: (0, jnp.maximum(j - na, 0))),
                  pl.BlockSpec((d, LANES), lambda i, j: (0, 0))],
        out_specs=(pl.BlockSpec((tm, tn), lambda i, j: (i, j)),
                   pl.BlockSpec((tm, LANES), lambda i, j: (i, 0))),
        scratch_shapes=[pltpu.VMEM((tm, d), BF16)],
        compiler_params=_params("parallel", "arbitrary"),
        name="in_proj",
    )(h, ada3, gain, w_all, wb, wba)


def _softplus(x):
    return jnp.maximum(x, 0.0) + jnp.log1p(jnp.exp(-jnp.abs(x)))


def _dn_kernel(q_ref, k_ref, v_ref, z_ref, ba_ref, bat_ref, wq_ref, wk_ref, wv_ref,
               arow_ref, dtrow_ref, acol_ref, dtcol_ref, gain_ref, o_ref,
               prev_ref, state_ref, gcrow_ref, *, hb):
    c = DN_CHUNK
    hg = pl.program_id(1)
    n = pl.program_id(2)

    @pl.when(n == 0)
    def _():
        prev_ref[...] = jnp.zeros_like(prev_ref)
        state_ref[...] = jnp.zeros_like(state_ref)

    width = hb * HEAD_DIM
    sub = prev_ref.shape[1]
    row_t = lax.broadcasted_iota(jnp.int32, (sub, width), 0)

    def conv_silu(x_ref, w_ref, s):
        cur = x_ref[0].astype(F32)
        tail = prev_ref[s]
        acc = cur * w_ref[DN_CONV - 1:DN_CONV, :]
        for d in range(1, DN_CONV):
            shifted = pltpu.roll(cur, d, 0)
            head = jnp.where(row_t < d, pltpu.roll(tail, d, 0), shifted[:sub])
            shifted = jnp.concatenate([head, shifted[sub:]], axis=0)
            acc = acc + shifted * w_ref[DN_CONV - 1 - d:DN_CONV - d, :]
        prev_ref[s] = cur[c - sub:]
        return _silu(acc)

    xq = conv_silu(q_ref, wq_ref, 0)
    xk = conv_silu(k_ref, wk_ref, 1)
    xv = conv_silu(v_ref, wv_ref, 2)

    slab = ba_ref[0]
    beta_mat = jax.nn.sigmoid(slab)
    g_mat = -jnp.exp(arow_ref[...]) * _softplus(slab + dtrow_ref[...])
    ri = lax.broadcasted_iota(jnp.int32, (c, c), 0)
    ci = lax.broadcasted_iota(jnp.int32, (c, c), 1)
    lower_incl = (ri >= ci)
    tri_l = jnp.where(lower_incl, 1.0, 0.0).astype(BF16)
    tri_u = jnp.where(ri <= ci, 1.0, 0.0).astype(BF16)

    def exact_ones_dot(x, ones, left):
        x1 = x.astype(BF16)
        r1 = x - x1.astype(F32)
        x2 = r1.astype(BF16)
        x3 = (r1 - x2.astype(F32)).astype(BF16)
        if left:
            f = lambda p: jnp.dot(ones, p, preferred_element_type=F32)
        else:
            f = lambda p: jnp.dot(p, ones, preferred_element_type=F32)
        return f(x1) + (f(x2) + f(x3))

    gc_colmat = exact_ones_dot(g_mat, tri_l, True)
    a_rows = bat_ref[0, HEADS:2 * HEADS, :]
    g_rows = -jnp.exp(acol_ref[...]) * _softplus(a_rows + dtcol_ref[...])
    gcrow_ref[...] = exact_ones_dot(g_rows, tri_u, False)

    lane = lax.broadcasted_iota(jnp.int32, (c, LANES), 1)
    eye = jnp.where(ri == ci, 1.0, 0.0)
    strict = ri > ci
    scale = HEAD_DIM ** -0.5

    hs = []
    for hh in range(hb):
        h = hg * hb + hh
        sl = slice(hh * HEAD_DIM, (hh + 1) * HEAD_DIM)
        beta = jnp.sum(jnp.where(lane == h, beta_mat, 0.0), axis=1, keepdims=True)
        gcc = jnp.sum(jnp.where(lane == HEADS + h, gc_colmat, 0.0), axis=1, keepdims=True)
        gcr = gcrow_ref[pl.ds(h, 1), :]
        gc_last = gcr[:, c - 1:c]
        decay = jnp.where(lower_incl, jnp.exp(jnp.minimum(gcc - gcr, 0.0)), 0.0)
        q = xq[:, sl]
        k = xk[:, sl]
        q = q * (lax.rsqrt(jnp.sum(q * q, axis=-1, keepdims=True) + NORM_EPS) * scale)
        k = k * lax.rsqrt(jnp.sum(k * k, axis=-1, keepdims=True) + NORM_EPS)
        kb = k * beta
        egc = jnp.exp(gcc)
        hs.append(dict(hh=hh, sl=sl, decay=decay, q=q, k=k, kb=kb,
                       rhs=jnp.concatenate([xv[:, sl] * beta, kb * egc], axis=1),
                       qd=q * egc, kd=k * jnp.exp(gc_last - gcc), g_tot=jnp.exp(gc_last)))

    for s in hs:
        s["lmat"] = jnp.where(strict, _dot(s["kb"], s["k"], NT_DIMS) * s["decay"], 0.0)
        s["qk"] = _dot(s["q"], s["k"], NT_DIMS) * s["decay"]
    shift0 = int(math.log2(DN_INV_BASE))
    mm = lambda a, b: jnp.dot(a, b, preferred_element_type=F32)
    same_block = jnp.where((ri >> shift0) == (ci >> shift0), 1.0, 0.0).astype(BF16)
    for s in hs:
        s["lb"] = s["lmat"].astype(BF16)
        s["pw"] = -(s["lb"] * same_block)
        s["tmat"] = eye + s["pw"].astype(F32)
    for _ in range(shift0 - 1):
        for s in hs:
            s["pw"] = mm(s["pw"], s["pw"]).astype(BF16)
        for s in hs:
            s["tmat"] = s["tmat"] + mm(s["tmat"].astype(BF16), s["pw"])
    for sh in range(shift0, int(math.log2(c))):
        lower_left = jnp.where(jnp.logical_and((ri >> sh) == (ci >> sh) + 1, ((ci >> sh) & 1) == 0),
                               1.0, 0.0).astype(BF16)
        for s in hs:
            s["tb"] = s["tmat"].astype(BF16)
            s["pw"] = mm(s["lb"] * lower_left, s["tb"]).astype(BF16)
        for s in hs:
            s["tmat"] = s["tmat"] - mm(s["tb"], s["pw"])
    for s in hs:
        s["uw"] = _dot(s["tmat"], s["rhs"])
    for s in hs:
        s["state"] = state_ref[s["hh"]]
        s["v_new"] = s["uw"][:, :HEAD_DIM] - _dot(s["uw"][:, HEAD_DIM:], s["state"])
    for s in hs:
        s["o"] = _dot(s["qd"], s["state"]) + _dot(s["qk"], s["v_new"])
        state_ref[s["hh"]] = s["state"] * s["g_tot"] + _dot(s["kd"], s["v_new"], TN_DIMS)
    for s in hs:
        o = s["o"]
        o = o * lax.rsqrt(jnp.mean(o * o, axis=-1, keepdims=True) + NORM_EPS) * gain_ref[...]
        o_ref[0, :, s["sl"]] = (o * _silu(z_ref[0, :, s["sl"]].astype(F32))).astype(o_ref.dtype)


def _deltanet(proj3, ba3, bat3, conv_w, a_log, dt_bias, gain):
    b, t, _ = proj3.shape
    hb = DN_HEAD_BLOCK
    c = DN_CHUNK
    width = hb * HEAD_DIM
    ng = HEADS // hb
    region = (HEADS * HEAD_DIM) // width

    def pad_row(x):
        return jnp.zeros((1, LANES), F32).at[0, HEADS:2 * HEADS].set(x.astype(F32))

    arow, dtrow = pad_row(a_log), pad_row(dt_bias)
    acol = a_log.astype(F32).reshape(HEADS, 1)
    dtcol = dt_bias.astype(F32).reshape(HEADS, 1)
    gain2 = gain.astype(F32).reshape(1, HEAD_DIM)

    def col(r):
        return pl.BlockSpec((1, c, width), lambda bi, g, n, r=r: (bi, n, r * region + g))

    def wcol(r):
        return pl.BlockSpec((DN_CONV, width), lambda bi, g, n, r=r: (0, r * region + g))

    const = lambda shape: pl.BlockSpec(shape, lambda bi, g, n: tuple(0 for _ in shape))
    return pl.pallas_call(
        functools.partial(_dn_kernel, hb=hb),
        out_shape=jax.ShapeDtypeStruct((b, t, HEADS * HEAD_DIM), BF16),
        grid=(b, ng, t // c),
        in_specs=[col(0), col(1), col(2), col(3),
                  pl.BlockSpec((1, c, LANES), lambda bi, g, n: (bi, n, 0)),
                  pl.BlockSpec((1, LANES, c), lambda bi, g, n: (bi, 0, n)),
                  wcol(0), wcol(1), wcol(2),
                  const((1, LANES)), const((1, LANES)), const((HEADS, 1)), const((HEADS, 1)),
                  const((1, HEAD_DIM))],
        out_specs=pl.BlockSpec((1, c, width), lambda bi, g, n: (bi, n, g)),
        scratch_shapes=[pltpu.VMEM((3, SUBLANES, width), F32),
                        pltpu.VMEM((hb, HEAD_DIM, HEAD_DIM), F32),
                        pltpu.VMEM((HEADS, c), F32)],
        compiler_params=_params("parallel", "parallel", "arbitrary"),
        name="deltanet",
    )(proj3, proj3, proj3, proj3, ba3, bat3, conv_w, conv_w, conv_w, arow, dtrow, acol, dtcol, gain2)


def _moba_prep_kernel(q_ref, k_ref, v_ref, qg_ref, kg_ref, qn_ref, kn_ref, vt_ref, km_ref):
    n = pl.program_id(1)
    scale = HEAD_DIM ** -0.5

    @pl.when(n == 0)
    def _():
        km_ref[...] = jnp.zeros_like(km_ref)

    row = lax.broadcasted_iota(jnp.int32, (km_ref.shape[1], HEAD_DIM), 0)
    ones = jnp.ones((MB_VROWS - HEAD_DIM, MB_BLOCK), BF16)
    lane = lax.broadcasted_iota(jnp.int32, (MB_BLOCK, LANES), 1)
    onehot = jnp.where(lane == n, 1.0, 0.0).astype(BF16)
    for h in range(HEADS):
        sl = slice(h * HEAD_DIM, (h + 1) * HEAD_DIM)
        ka = 2 * h * HEAD_DIM
        q = q_ref[0, :, sl].astype(F32)
        k = k_ref[0, :, sl].astype(F32)
        qn = q * lax.rsqrt(jnp.mean(q * q, axis=-1, keepdims=True) + NORM_EPS) * qg_ref[...]
        kn = k * lax.rsqrt(jnp.mean(k * k, axis=-1, keepdims=True) + NORM_EPS) * kg_ref[...]
        qn_ref[0, :, sl] = (qn * scale).astype(qn_ref.dtype)
        kn_ref[0, 0, :, ka:ka + HEAD_DIM] = kn.astype(BF16)
        kn_ref[0, 0, :, ka + HEAD_DIM:ka + 2 * HEAD_DIM] = onehot
        km_ref[0, :, sl] = jnp.where(row == n, jnp.mean(kn, axis=0, keepdims=True), km_ref[0, :, sl])
        vt_ref[0, 0, h, :HEAD_DIM, :] = v_ref[0, :, sl].astype(F32).T.astype(BF16)
        vt_ref[0, 0, h, HEAD_DIM:, :] = ones


def _moba_prep(proj3, q_gain, k_gain):
    b, t, _ = proj3.shape
    nb = t // MB_BLOCK
    w = HEADS * HEAD_DIM
    col = lambda r: pl.BlockSpec((1, MB_BLOCK, w), lambda bi, n, r=r: (bi, n, r))
    gspec = pl.BlockSpec((1, HEAD_DIM), lambda bi, n: (0, 0))
    return pl.pallas_call(
        _moba_prep_kernel,
        out_shape=(jax.ShapeDtypeStruct((b, t, w), BF16),
                   jax.ShapeDtypeStruct((b, nb, MB_BLOCK, 2 * w), BF16),
                   jax.ShapeDtypeStruct((b, nb, HEADS, MB_VROWS, MB_BLOCK), BF16),
                   jax.ShapeDtypeStruct((b, nb, w), F32)),
        grid=(b, nb),
        in_specs=[col(4), col(5), col(6), gspec, gspec],
        out_specs=(pl.BlockSpec((1, MB_BLOCK, w), lambda bi, n: (bi, n, 0)),
                   pl.BlockSpec((1, 1, MB_BLOCK, 2 * w), lambda bi, n: (bi, n, 0, 0)),
                   pl.BlockSpec((1, 1, HEADS, MB_VROWS, MB_BLOCK), lambda bi, n: (bi, n, 0, 0, 0)),
                   pl.BlockSpec((1, nb, w), lambda bi, n: (bi, 0, 0))),
        compiler_params=_params("parallel", "arbitrary"),
        name="moba_prep",
    )(proj3, proj3, proj3, q_gain.astype(F32).reshape(1, HEAD_DIM), k_gain.astype(F32).reshape(1, HEAD_DIM))


def _bucket_thresholds():
    max_exact = RP_BUCKETS // 2
    th = list(range(1, max_exact + 1))
    for kk in range(1, RP_BUCKETS - max_exact):
        x = max_exact * (RP_MAX_DIST / max_exact) ** (kk / (RP_BUCKETS - max_exact))
        r = round(x)
        th.append(r if abs(x - r) < 1e-9 else math.ceil(x))
    return th


N_NEAR = 5
LOG2E = 1.4426950408889634


def _moba_kernel(rb_ref, q_ref, k_ref, vt_ref, km_ref, o_ref, tab_ref, qa_ref, m_ref, alpha_ref, acc_ref, p_ref,
                 *, nb, hb):
    hg = pl.program_id(0)
    bi = pl.program_id(1)
    it = pl.program_id(2)
    blk = MB_BLOCK
    nq = MB_QBLOCKS
    tq = nq * blk
    shift = int(math.log2(blk))
    last = nq * it + nq - 1
    thresholds = _bucket_thresholds()
    heads = list(range(hb))
    hsl = lambda hh: slice(hh * HEAD_DIM, (hh + 1) * HEAD_DIM)

    @pl.when(jnp.logical_and(bi == 0, it == 0))
    def _():
        ko = lax.broadcasted_iota(jnp.int32, (blk, blk), 0)
        qo = lax.broadcasted_iota(jnp.int32, (blk, blk), 1)
        for hh in heads:
            h = hg * hb + hh
            far = rb_ref[RP_BUCKETS - 1, h]
            for delta in range(N_NEAR):
                d = delta * blk + qo - ko
                acc = jnp.full((blk, blk), rb_ref[0, h], F32)
                for bkt in range(1, RP_BUCKETS):
                    if delta * blk + blk - 1 >= thresholds[bkt - 1]:
                        acc = jnp.where(d >= thresholds[bkt - 1], rb_ref[bkt, h], acc)
                acc = (acc - far) * LOG2E
                if delta == 0:
                    acc = jnp.where(d >= 0, acc, NEG_BIG)
                tab_ref[hh, delta] = acc
            tab_ref[hh, N_NEAR] = jnp.zeros((blk, blk), F32)

    bidx = lax.broadcasted_iota(jnp.int32, (nb, tq), 0)
    qblk = nq * it + (lax.broadcasted_iota(jnp.int32, (nb, tq), 1) >> shift)
    past = bidx < qblk
    gates = []
    for hh in heads:
        kmh, kml = _split_bf16(km_ref[0, :, hsl(hh)])
        qb = q_ref[0, :, hsl(hh)]
        gate = (lax.dot_general(kmh, qb, NT_DIMS, preferred_element_type=F32)
                + lax.dot_general(kml, qb, NT_DIMS, preferred_element_type=F32))
        gates.append(jnp.where(past, gate, -jnp.inf))
    for hh, gate in zip(heads, gates):
        rank = jnp.zeros((nb, tq), jnp.int32)
        for jp in range(nb):
            gj = gate[jp:jp + 1, :]
            ahead = jnp.logical_or(gj > gate, jnp.logical_and(gj == gate, jp < bidx))
            rank = rank + ahead.astype(jnp.int32)
        allowed = jnp.logical_or(jnp.logical_and(past, rank < MB_TOPK), bidx == qblk)
        mask = jnp.where(allowed, 0.0, NEG_BIG)
        q_t = (q_ref[0, :, hsl(hh)].astype(F32) * LOG2E).T.astype(BF16)
        mask_rows = jnp.concatenate([mask, jnp.zeros((LANES - nb, tq), F32)], axis=0).astype(BF16)
        qa_ref[hh] = jnp.concatenate([q_t, mask_rows], axis=0)

    m_ref[...] = jnp.full(m_ref.shape, NEG_BIG, F32)
    acc_ref[...] = jnp.zeros(acc_ref.shape, F32)

    def logits(t, near, hs):
        j = last - t
        ss = [jnp.dot(k_ref[0, j, :, 2 * hh * HEAD_DIM:2 * (hh + 1) * HEAD_DIM], qa_ref[hh],
                      preferred_element_type=F32) for hh in hs]
        if near:
            idx = [jnp.clip(t - (nq - 1 - a), 0, N_NEAR) for a in range(nq)]
            ss = [jnp.concatenate([s[:, a * blk:(a + 1) * blk] + tab_ref[hh, idx[a]] for a in range(nq)], axis=1)
                  for hh, s in zip(hs, ss)]
        return ss

    def values(t, hs):
        slot = t & 1
        return [jnp.dot(vt_ref[0, last - t, hh], p_ref[slot, hh], preferred_element_type=F32) for hh in hs]

    def probabilities(t, ss, hs):
        slot = t & 1
        alphas = []
        for hh, s in zip(hs, ss):
            for q0 in range(0, tq, LANES):
                qs = slice(q0, q0 + LANES)
                m_old = m_ref[hh, :, qs]
                m_new = jnp.maximum(m_old, jnp.max(s[:, qs], axis=0, keepdims=True))
                m_ref[hh, :, qs] = m_new
                p_ref[slot, hh, :, qs] = jnp.exp2(s[:, qs] - m_new).astype(BF16)
                alphas.append((hh, qs, jnp.exp2(m_old - m_new)))
        return alphas

    def accumulate(pvs, hs):
        for hh, pv in zip(hs, pvs):
            acc_ref[hh] = alpha_ref[hh] * acc_ref[hh] + pv

    def set_alpha(alphas):
        for hh, qs, alpha in alphas:
            alpha_ref[hh, :, qs] = alpha

    groups = [heads[g:g + MB_ISSUE_GROUP] for g in range(0, hb, MB_ISSUE_GROUP)]

    def make_body(near):
        def body(t, carry):
            ahead = [logits(t, near, grp) for grp in groups[:MB_ISSUE_AHEAD]]
            for gi, grp in enumerate(groups):
                if gi + MB_ISSUE_AHEAD < len(groups):
                    ahead.append(logits(t, near, groups[gi + MB_ISSUE_AHEAD]))
                pvs = values(t - 1, grp)
                alphas = probabilities(t, ahead[gi], grp)
                accumulate(pvs, grp)
                set_alpha(alphas)
            return carry
        return body

    set_alpha(probabilities(0, logits(0, True, heads), heads))
    n_near = jnp.minimum(last + 1, N_NEAR + nq - 1)
    lax.fori_loop(1, n_near, make_body(True), 0)
    lax.fori_loop(n_near, last + 1, make_body(False), 0)
    accumulate(values(last, heads), heads)
    for hh in heads:
        acc = acc_ref[hh]
        o_ref[0, :, hsl(hh)] = (acc[:HEAD_DIM] / acc[HEAD_DIM:HEAD_DIM + 1]).T.astype(o_ref.dtype)


def _moba(qn, kn, vt, kmean, rel_bias):
    b, t, w = qn.shape
    nb = t // MB_BLOCK
    hb = MB_HEAD_BLOCK
    hw = hb * HEAD_DIM
    tq = MB_QBLOCKS * MB_BLOCK
    return pl.pallas_call(
        functools.partial(_moba_kernel, nb=nb, hb=hb),
        out_shape=jax.ShapeDtypeStruct((b, t, w), BF16),
        grid=(HEADS // hb, b, t // tq),
        in_specs=[pl.BlockSpec(memory_space=pltpu.SMEM),
                  pl.BlockSpec((1, tq, hw), lambda g, bi, i: (bi, i, g)),
                  pl.BlockSpec((1, nb, MB_BLOCK, 2 * hw), lambda g, bi, i: (bi, 0, 0, g),
                               pipeline_mode=pl.Buffered(1)),
                  pl.BlockSpec((1, nb, hb, MB_VROWS, MB_BLOCK), lambda g, bi, i: (bi, 0, g, 0, 0),
                               pipeline_mode=pl.Buffered(1)),
                  pl.BlockSpec((1, nb, hw), lambda g, bi, i: (bi, 0, g))],
        out_specs=pl.BlockSpec((1, tq, hw), lambda g, bi, i: (bi, i, g)),
        scratch_shapes=[pltpu.VMEM((hb, N_NEAR + 1, MB_BLOCK, MB_BLOCK), F32),
                        pltpu.VMEM((hb, 2 * HEAD_DIM, tq), BF16),
                        pltpu.VMEM((hb, 1, tq), F32),
                        pltpu.VMEM((hb, 1, tq), F32),
                        pltpu.VMEM((hb, MB_VROWS, tq), F32),
                        pltpu.VMEM((2, hb, MB_BLOCK, tq), BF16)],
        compiler_params=_params("arbitrary", "arbitrary", "arbitrary"),
        name="moba_attn",
    )(rel_bias.astype(F32), qn, kn, vt, kmean)


def _merge_kernel(ya_ref, yb_ref, wa_ref, wb_ref, ga_ref, gb_ref, o_ref):
    pa = jnp.dot(ya_ref[...], wa_ref[...], preferred_element_type=F32)
    pb = jnp.dot(yb_ref[...], wb_ref[...], preferred_element_type=F32)
    ga = ga_ref[...].astype(F32)
    gb = gb_ref[...].astype(F32)
    o_ref[...] = (jax.nn.sigmoid(ga) * pa + jax.nn.sigmoid(gb) * pb).astype(o_ref.dtype)


def _merge(ya, yb, wa, wb, proj, gate_col):
    m, d = ya.shape
    tm, tn = 1024, 512
    gpb = d // tn
    return pl.pallas_call(
        _merge_kernel,
        out_shape=jax.ShapeDtypeStruct((m, d), BF16),
        grid=(m // tm, d // tn),
        in_specs=[pl.BlockSpec((tm, d), lambda i, j: (i, 0)),
                  pl.BlockSpec((tm, d), lambda i, j: (i, 0)),
                  pl.BlockSpec((d, tn), lambda i, j: (0, j)),
                  pl.BlockSpec((d, tn), lambda i, j: (0, j)),
                  pl.BlockSpec((tm, tn), lambda i, j: (i, gate_col * gpb + j)),
                  pl.BlockSpec((tm, tn), lambda i, j: (i, (gate_col + 1) * gpb + j))],
        out_specs=pl.BlockSpec((tm, tn), lambda i, j: (i, j)),
        compiler_params=_params("parallel", "arbitrary"),
        name="merge",
    )(ya, yb, wa, wb, proj, proj)


def _outproj_kernel(x_ref, w_ref, h_ref, ada_ref, o_ref):
    o_ref[...] = h_ref[...] + ada_ref[0] * jnp.dot(x_ref[...], w_ref[...], preferred_element_type=F32)


def _outproj(x, w, h, gate3, *, seq):
    m, d = x.shape
    tm, tn = 1024, 512
    per_batch = seq // tm
    return pl.pallas_call(
        _outproj_kernel,
        out_shape=jax.ShapeDtypeStruct((m, d), F32),
        grid=(m // tm, d // tn),
        in_specs=[pl.BlockSpec((tm, d), lambda i, j: (i, 0)),
                  pl.BlockSpec((d, tn), lambda i, j: (0, j)),
                  pl.BlockSpec((tm, tn), lambda i, j: (i, j)),
                  pl.BlockSpec((1, 1, tn), lambda i, j: (i // per_batch, 0, j))],
        out_specs=pl.BlockSpec((tm, tn), lambda i, j: (i, j)),
        compiler_params=_params("parallel", "arbitrary"),
        name="out_proj",
    )(x, w, h, gate3)


def kernel(x, c, ada_w, ada_b, norm1_g, ffn1_w1, ffn1_w3, ffn1_w2, norm2_g, w_in, dn_conv_w, dn_a_log, dn_dt_bias, dn_norm_g, mb_q_norm_g, mb_k_norm_g, rel_bias, w_proj_a, w_proj_b, w_out, norm3_g, ffn2_w1, ffn2_w3, ffn2_w2):
    b, t, d = x.shape
    m = b * t
    depth = ada_w.shape[0]
    wide = HEADS * HEAD_DIM
    cut0 = 4 * wide
    cut1 = cut0 + 2 * HEADS
    h = x.reshape(m, d)
    c_pad = jnp.zeros((8, d), F32).at[:b].set(c)
    for l in range(depth):
        ada = _ada(c_pad, ada_w, ada_b.reshape(depth, 1, -1), l)[:b]
        ada3 = ada.reshape(b, N_ADA, d)
        h = _ffn(h, ada3, norm1_g[l].reshape(1, d), ffn1_w1[l].astype(BF16), ffn1_w3[l].astype(BF16),
                 ffn1_w2[l].astype(BF16), slot=0, seq=t)

        w_in16 = w_in.astype(BF16)
        w_tail = w_in16[l][:, cut1:]
        w_ba = jnp.zeros((d, LANES), BF16).at[:, :2 * HEADS].set(w_in16[l][:, cut0:cut1])
        proj, ba = _inproj(h, ada3, norm2_g[l].reshape(1, d), w_in16, l, cut0, w_tail, w_ba, seq=t)
        proj3 = proj.reshape(b, t, -1)
        ba3 = ba.reshape(b, t, LANES)
        bat3 = jnp.transpose(ba3, (0, 2, 1))
        y_a = _deltanet(proj3, ba3, bat3, dn_conv_w[l], dn_a_log[l], dn_dt_bias[l], dn_norm_g[l])
        qn, kn, vt, kmean = _moba_prep(proj3, mb_q_norm_g[l], mb_k_norm_g[l])
        y_b = _moba(qn, kn, vt, kmean, rel_bias)
        merged = _merge(y_a.reshape(m, wide), y_b.reshape(m, wide), w_proj_a[l].astype(BF16),
                        w_proj_b[l].astype(BF16), proj, 7)
        gate2 = ada3[:, 5:6, :]
        h = _outproj(merged, w_out[l].astype(BF16), h, gate2, seq=t)

        h = _ffn(h, ada3, norm3_g[l].reshape(1, d), ffn2_w1[l].astype(BF16), ffn2_w3[l].astype(BF16),
                 ffn2_w2[l].astype(BF16), slot=6, seq=t)
    return h.reshape(b, t, d)
```

```python
import functools
import math

import jax
import jax.numpy as jnp
from jax import lax
from jax.experimental import pallas as pl
from jax.experimental.pallas import tpu as pltpu

F32 = jnp.float32
BF16 = jnp.bfloat16

LANES = 128
BF16_ROWS = 16
NORM_EPS = 1e-6
N_ADA = 9
HEADS = 16
HEAD_DIM = 128
DN_CONV = 4
DN_CHUNK = 128
DN_HEAD_BLOCK = 16
DN_INV_BASE = 8
MB_BLOCK = 256
MB_TOPK = 3
MB_HEAD_BLOCK = 8
MB_QBLOCKS = 1
MB_ISSUE_GROUP = 1
MB_ISSUE_AHEAD = 2
MB_VROWS = HEAD_DIM + 16
RP_BUCKETS = 32
RP_MAX_DIST = 1024
NEG_BIG = -1e30
VMEM_LIMIT = 56 * 1024 * 1024

NT_DIMS = (((1,), (1,)), ((), ()))
NN_DIMS = (((1,), (0,)), ((), ()))
TN_DIMS = (((0,), (0,)), ((), ()))


def _dot(a, b, dims=NN_DIMS):
    return lax.dot_general(a.astype(BF16), b.astype(BF16), dims, preferred_element_type=F32)


def _split_bf16(x):
    hi = x.astype(BF16)
    lo = (x - hi.astype(F32)).astype(BF16)
    return hi, lo


def _dot3(a, b, dims=NN_DIMS):
    ah, al = _split_bf16(a)
    bh, bl = _split_bf16(b)
    f = lambda x, y: lax.dot_general(x, y, dims, preferred_element_type=F32)
    return f(ah, bh) + (f(ah, bl) + f(al, bh))


def _silu(x):
    return x * jax.nn.sigmoid(x)


def _params(*sem):
    return pltpu.CompilerParams(dimension_semantics=sem, vmem_limit_bytes=VMEM_LIMIT)


def _ada_kernel(c_ref, w_ref, b_ref, o_ref):
    s = _silu(c_ref[...])
    o_ref[...] = _dot3(s, w_ref[0]) + b_ref[0]


def _ada(c_pad, w, b, layer):
    rows, d = c_pad.shape
    n = w.shape[2]
    tn = 1024
    return pl.pallas_call(
        _ada_kernel,
        out_shape=jax.ShapeDtypeStruct((rows, n), F32),
        grid=(n // tn,),
        in_specs=[pl.BlockSpec((rows, d), lambda j: (0, 0)),
                  pl.BlockSpec((1, d, tn), lambda j: (layer, 0, j)),
                  pl.BlockSpec((1, 1, tn), lambda j: (layer, 0, j))],
        out_specs=pl.BlockSpec((rows, tn), lambda j: (0, j)),
        compiler_params=_params("arbitrary"),
        name="ada_proj",
    )(c_pad, w, b)


def _norm_modulate(h, gain, shift, scale):
    ms = jnp.mean(h * h, axis=-1, keepdims=True)
    return (h * lax.rsqrt(ms + NORM_EPS)) * (gain * (1.0 + scale)) + shift


def _ffn_kernel(h_ref, ada_ref, g_ref, w1_ref, w3_ref, w2_ref, o_ref, u_ref, *, slot, nf):
    f = pl.program_id(1)

    @pl.when(f == 0)
    def _():
        u = _norm_modulate(h_ref[...], g_ref[...], ada_ref[0, slot:slot + 1, :], ada_ref[0, slot + 1:slot + 2, :])
        u_ref[...] = u.astype(BF16)
        o_ref[...] = jnp.zeros_like(o_ref)

    u = u_ref[...]
    a = jnp.dot(u, w1_ref[...], preferred_element_type=F32)
    b = jnp.dot(u, w3_ref[...], preferred_element_type=F32)
    o_ref[...] += jnp.dot((_silu(a) * b).astype(BF16), w2_ref[...], preferred_element_type=F32)

    @pl.when(f == nf - 1)
    def _():
        o_ref[...] = h_ref[...] + 0.5 * ada_ref[0, slot + 2:slot + 3, :] * o_ref[...]


def _ffn(h, ada3, gain, w1, w3, w2, *, slot, seq):
    m, d = h.shape
    dff = w1.shape[1]
    tm, tf = 1024, 512
    nf = dff // tf
    per_batch = seq // tm
    return pl.pallas_call(
        functools.partial(_ffn_kernel, slot=slot, nf=nf),
        out_shape=jax.ShapeDtypeStruct((m, d), F32),
        grid=(m // tm, nf),
        in_specs=[pl.BlockSpec((tm, d), lambda i, f: (i, 0), pipeline_mode=pl.Buffered(1)),
                  pl.BlockSpec((1, N_ADA, d), lambda i, f: (i // per_batch, 0, 0)),
                  pl.BlockSpec((1, d), lambda i, f: (0, 0)),
                  pl.BlockSpec((d, tf), lambda i, f: (0, f)),
                  pl.BlockSpec((d, tf), lambda i, f: (0, f)),
                  pl.BlockSpec((tf, d), lambda i, f: (f, 0))],
        out_specs=pl.BlockSpec((tm, d), lambda i, f: (i, 0)),
        scratch_shapes=[pltpu.VMEM((tm, d), BF16)],
        compiler_params=_params("parallel", "arbitrary"),
        name="ffn",
    )(h, ada3, gain, w1, w3, w2)


def _inproj_kernel(h_ref, ada_ref, g_ref, wa_ref, wb_ref, wba_ref, o_ref, ba_ref, u_ref, *, na):
    j = pl.program_id(1)

    @pl.when(j == 0)
    def _():
        u = _norm_modulate(h_ref[...], g_ref[...], ada_ref[0, 3:4, :], ada_ref[0, 4:5, :])
        u_ref[...] = u.astype(BF16)
        ba_ref[...] = jnp.dot(u_ref[...], wba_ref[...], preferred_element_type=F32)

    @pl.when(j < na)
    def _():
        o_ref[...] = jnp.dot(u_ref[...], wa_ref[0], preferred_element_type=F32).astype(o_ref.dtype)

    @pl.when(j >= na)
    def _():
        o_ref[...] = jnp.dot(u_ref[...], wb_ref[...], preferred_element_type=F32).astype(o_ref.dtype)


def _inproj(h, ada3, gain, w_all, layer, n_lead, wb, wba, *, seq):
    m, d = h.shape
    tm, tn = 1024, 1024
    na = n_lead // tn
    n = n_lead + wb.shape[1]
    per_batch = seq // tm
    return pl.pallas_call(
        functools.partial(_inproj_kernel, na=na),
        out_shape=(jax.ShapeDtypeStruct((m, n), BF16), jax.ShapeDtypeStruct((m, LANES), F32)),
        grid=(m // tm, n // tn),
        in_specs=[pl.BlockSpec((tm, d), lambda i, j: (i, 0)),
                  pl.BlockSpec((1, N_ADA, d), lambda i, j: (i // per_batch, 0, 0)),
                  pl.BlockSpec((1, d), lambda i, j: (0, 0)),
                  pl.BlockSpec((1, d, tn), lambda i, j: (layer, 0, jnp.minimum(j, na - 1))),
                  pl.BlockSpec((d, tn), lambda i, j: (0, jnp.maximum(j - na, 0))),
                  pl.BlockSpec((d, LANES), lambda i, j: (0, 0))],
        out_specs=(pl.BlockSpec((tm, tn), lambda i, j: (i, j)),
                   pl.BlockSpec((tm, LANES), lambda i, j: (i, 0))),
        scratch_shapes=[pltpu.VMEM((tm, d), BF16)],
        compiler_params=_params("parallel", "arbitrary"),
        name="in_proj",
    )(h, ada3, gain, w_all, wb, wba)


def _softplus(x):
    return jnp.maximum(x, 0.0) + jnp.log1p(jnp.exp(-jnp.abs(x)))


def _dn_kernel(q_ref, k_ref, v_ref, z_ref, ba_ref, bat_ref, wq_ref, wk_ref, wv_ref,
               arow_ref, dtrow_ref, acol_ref, dtcol_ref, gain_ref, o_ref,
               prev_ref, state_ref, gcrow_ref, *, hb):
    c = DN_CHUNK
    hg = pl.program_id(1)
    n = pl.program_id(2)

    @pl.when(n == 0)
    def _():
        prev_ref[...] = jnp.zeros_like(prev_ref)
        state_ref[...] = jnp.zeros_like(state_ref)

    nt = prev_ref.shape[1]
    st = lax.broadcasted_iota(jnp.int32, (c, c + nt), 0)
    ss_ = lax.broadcasted_iota(jnp.int32, (c, c + nt), 1)
    shifts = jnp.concatenate(
        [jnp.where(ss_ == jnp.where(st >= d, st - d, c + nt - d + st), 1.0, 0.0).astype(BF16)
         for d in range(1, DN_CONV)], axis=0)

    def conv_silu(x_ref, w_ref, s):
        cur = x_ref[0].astype(BF16)
        ext = jnp.concatenate([cur, prev_ref[s]], axis=0)
        taps = jnp.dot(shifts, ext, preferred_element_type=F32)
        acc = cur.astype(F32) * w_ref[DN_CONV - 1:DN_CONV, :]
        for d in range(1, DN_CONV):
            acc = acc + taps[(d - 1) * c:d * c] * w_ref[DN_CONV - 1 - d:DN_CONV - d, :]
        prev_ref[s] = cur[c - nt:]
        return _silu(acc)

    xq = conv_silu(q_ref, wq_ref, 0)
    xk = conv_silu(k_ref, wk_ref, 1)
    xv = conv_silu(v_ref, wv_ref, 2)

    slab = ba_ref[0]
    beta_mat = jax.nn.sigmoid(slab)
    g_mat = -jnp.exp(arow_ref[...]) * _softplus(slab + dtrow_ref[...])
    ri = lax.broadcasted_iota(jnp.int32, (c, c), 0)
    ci = lax.broadcasted_iota(jnp.int32, (c, c), 1)
    lower_incl = (ri >= ci)
    tri_l = jnp.where(lower_incl, 1.0, 0.0).astype(BF16)
    tri_u = jnp.where(ri <= ci, 1.0, 0.0).astype(BF16)

    def exact_ones_dot(x, ones, left):
        x1 = x.astype(BF16)
        r1 = x - x1.astype(F32)
        x2 = r1.astype(BF16)
        x3 = (r1 - x2.astype(F32)).astype(BF16)
        if left:
            f = lambda p: jnp.dot(ones, p, preferred_element_type=F32)
        else:
            f = lambda p: jnp.dot(p, ones, preferred_element_type=F32)
        return f(x1) + (f(x2) + f(x3))

    gc_colmat = exact_ones_dot(g_mat, tri_l, True)
    a_rows = bat_ref[0, HEADS:2 * HEADS, :]
    g_rows = -jnp.exp(acol_ref[...]) * _softplus(a_rows + dtcol_ref[...])
    gcrow_ref[...] = exact_ones_dot(g_rows, tri_u, False)

    lane = lax.broadcasted_iota(jnp.int32, (c, LANES), 1)
    eye = jnp.where(ri == ci, 1.0, 0.0)
    strict = ri > ci
    scale = HEAD_DIM ** -0.5

    hs = []
    for hh in range(hb):
        h = hg * hb + hh
        sl = slice(hh * HEAD_DIM, (hh + 1) * HEAD_DIM)
        beta = jnp.sum(jnp.where(lane == h, beta_mat, 0.0), axis=1, keepdims=True)
        gcc = jnp.sum(jnp.where(lane == HEADS + h, gc_colmat, 0.0), axis=1, keepdims=True)
        gcr = gcrow_ref[pl.ds(h, 1), :]
        gc_last = gcr[:, c - 1:c]
        decay = jnp.where(lower_incl, jnp.exp(jnp.minimum(gcc - gcr, 0.0)), 0.0)
        q = xq[:, sl]
        k = xk[:, sl]
        q = q * (lax.rsqrt(jnp.sum(q * q, axis=-1, keepdims=True) + NORM_EPS) * scale)
        k = k * lax.rsqrt(jnp.sum(k * k, axis=-1, keepdims=True) + NORM_EPS)
        kb = k * beta
        egc = jnp.exp(gcc)
        hs.append(dict(hh=hh, sl=sl, decay=decay, q=q, k=k, kb=kb,
                       rhs=jnp.concatenate([xv[:, sl] * beta, kb * egc], axis=1),
                       qd=q * egc, kd=k * jnp.exp(gc_last - gcc), g_tot=jnp.exp(gc_last)))

    for s in hs:
        s["lmat"] = jnp.where(strict, _dot(s["kb"], s["k"], NT_DIMS) * s["decay"], 0.0)
        s["qk"] = _dot(s["q"], s["k"], NT_DIMS) * s["decay"]
    shift0 = int(math.log2(DN_INV_BASE))
    mm = lambda a, b: jnp.dot(a, b, preferred_element_type=F32)
    same_block = jnp.where((ri >> shift0) == (ci >> shift0), 1.0, 0.0).astype(BF16)
    for s in hs:
        s["lb"] = s["lmat"].astype(BF16)
        s["pw"] = -(s["lb"] * same_block)
        s["tmat"] = eye + s["pw"].astype(F32)
    for _ in range(shift0 - 1):
        for s in hs:
            s["pw"] = mm(s["pw"], s["pw"]).astype(BF16)
        for s in hs:
            s["tmat"] = s["tmat"] + mm(s["tmat"].astype(BF16), s["pw"])
    for sh in range(shift0, int(math.log2(c))):
        lower_left = jnp.where(jnp.logical_and((ri >> sh) == (ci >> sh) + 1, ((ci >> sh) & 1) == 0),
                               1.0, 0.0).astype(BF16)
        for s in hs:
            s["tb"] = s["tmat"].astype(BF16)
            s["pw"] = mm(s["lb"] * lower_left, s["tb"]).astype(BF16)
        for s in hs:
            s["tmat"] = s["tmat"] - mm(s["tb"], s["pw"])
    for s in hs:
        s["uw"] = _dot(s["tmat"], s["rhs"])
    for s in hs:
        s["state"] = state_ref[s["hh"]]
        s["v_new"] = s["uw"][:, :HEAD_DIM] - _dot(s["uw"][:, HEAD_DIM:], s["state"])
    for s in hs:
        s["o"] = _dot(s["qd"], s["state"]) + _dot(s["qk"], s["v_new"])
        state_ref[s["hh"]] = s["state"] * s["g_tot"] + _dot(s["kd"], s["v_new"], TN_DIMS)
    for s in hs:
        o = s["o"]
        o = o * lax.rsqrt(jnp.mean(o * o, axis=-1, keepdims=True) + NORM_EPS) * gain_ref[...]
        o_ref[0, :, s["sl"]] = (o * _silu(z_ref[0, :, s["sl"]].astype(F32))).astype(o_ref.dtype)


def _deltanet(proj3, ba3, bat3, conv_w, a_log, dt_bias, gain):
    b, t, _ = proj3.shape
    hb = DN_HEAD_BLOCK
    c = DN_CHUNK
    width = hb * HEAD_DIM
    ng = HEADS // hb
    region = (HEADS * HEAD_DIM) // width

    def pad_row(x):
        return jnp.zeros((1, LANES), F32).at[0, HEADS:2 * HEADS].set(x.astype(F32))

    arow, dtrow = pad_row(a_log), pad_row(dt_bias)
    acol = a_log.astype(F32).reshape(HEADS, 1)
    dtcol = dt_bias.astype(F32).reshape(HEADS, 1)
    gain2 = gain.astype(F32).reshape(1, HEAD_DIM)

    def col(r):
        return pl.BlockSpec((1, c, width), lambda bi, g, n, r=r: (bi, n, r * region + g))

    def wcol(r):
        return pl.BlockSpec((DN_CONV, width), lambda bi, g, n, r=r: (0, r * region + g))

    const = lambda shape: pl.BlockSpec(shape, lambda bi, g, n: tuple(0 for _ in shape))
    return pl.pallas_call(
        functools.partial(_dn_kernel, hb=hb),
        out_shape=jax.ShapeDtypeStruct((b, t, HEADS * HEAD_DIM), BF16),
        grid=(b, ng, t // c),
        in_specs=[col(0), col(1), col(2), col(3),
                  pl.BlockSpec((1, c, LANES), lambda bi, g, n: (bi, n, 0)),
                  pl.BlockSpec((1, LANES, c), lambda bi, g, n: (bi, 0, n)),
                  wcol(0), wcol(1), wcol(2),
                  const((1, LANES)), const((1, LANES)), const((HEADS, 1)), const((HEADS, 1)),
                  const((1, HEAD_DIM))],
        out_specs=pl.BlockSpec((1, c, width), lambda bi, g, n: (bi, n, g)),
        scratch_shapes=[pltpu.VMEM((3, BF16_ROWS, width), BF16),
                        pltpu.VMEM((hb, HEAD_DIM, HEAD_DIM), F32),
                        pltpu.VMEM((HEADS, c), F32)],
        compiler_params=_params("parallel", "parallel", "arbitrary"),
        name="deltanet",
    )(proj3, proj3, proj3, proj3, ba3, bat3, conv_w, conv_w, conv_w, arow, dtrow, acol, dtcol, gain2)


def _moba_prep_kernel(q_ref, k_ref, v_ref, qg_ref, kg_ref, qn_ref, kn_ref, vt_ref, km_ref):
    n = pl.program_id(1)
    scale = HEAD_DIM ** -0.5

    @pl.when(n == 0)
    def _():
        km_ref[...] = jnp.zeros_like(km_ref)

    row = lax.broadcasted_iota(jnp.int32, (km_ref.shape[1], HEAD_DIM), 0)
    ones = jnp.ones((MB_VROWS - HEAD_DIM, MB_BLOCK), BF16)
    lane = lax.broadcasted_iota(jnp.int32, (MB_BLOCK, LANES), 1)
    onehot = jnp.where(lane == n, 1.0, 0.0).astype(BF16)
    for h in range(HEADS):
        sl = slice(h * HEAD_DIM, (h + 1) * HEAD_DIM)
        ka = 2 * h * HEAD_DIM
        q = q_ref[0, :, sl].astype(F32)
        k = k_ref[0, :, sl].astype(F32)
        qn = q * lax.rsqrt(jnp.mean(q * q, axis=-1, keepdims=True) + NORM_EPS) * qg_ref[...]
        kn = k * lax.rsqrt(jnp.mean(k * k, axis=-1, keepdims=True) + NORM_EPS) * kg_ref[...]
        qn_ref[0, :, sl] = (qn * scale).astype(qn_ref.dtype)
        kn_ref[0, 0, :, ka:ka + HEAD_DIM] = kn.astype(BF16)
        kn_ref[0, 0, :, ka + HEAD_DIM:ka + 2 * HEAD_DIM] = onehot
        km_ref[0, :, sl] = jnp.where(row == n, jnp.mean(kn, axis=0, keepdims=True), km_ref[0, :, sl])
        vt_ref[0, 0, h, :HEAD_DIM, :] = v_ref[0, :, sl].astype(F32).T.astype(BF16)
        vt_ref[0, 0, h, HEAD_DIM:, :] = ones


def _moba_prep(proj3, q_gain, k_gain):
    b, t, _ = proj3.shape
    nb = t // MB_BLOCK
    w = HEADS * HEAD_DIM
    col = lambda r: pl.BlockSpec((1, MB_BLOCK, w), lambda bi, n, r=r: (bi, n, r))
    gspec = pl.BlockSpec((1, HEAD_DIM), lambda bi, n: (0, 0))
    return pl.pallas_call(
        _moba_prep_kernel,
        out_shape=(jax.ShapeDtypeStruct((b, t, w), BF16),
                   jax.ShapeDtypeStruct((b, nb, MB_BLOCK, 2 * w), BF16),
                   jax.ShapeDtypeStruct((b, nb, HEADS, MB_VROWS, MB_BLOCK), BF16),
                   jax.ShapeDtypeStruct((b, nb, w), F32)),
        grid=(b, nb),
        in_specs=[col(4), col(5), col(6), gspec, gspec],
        out_specs=(pl.BlockSpec((1, MB_BLOCK, w), lambda bi, n: (bi, n, 0)),
                   pl.BlockSpec((1, 1, MB_BLOCK, 2 * w), lambda bi, n: (bi, n, 0, 0)),
                   pl.BlockSpec((1, 1, HEADS, MB_VROWS, MB_BLOCK), lambda bi, n: (bi, n, 0, 0, 0)),
                   pl.BlockSpec((1, nb, w), lambda bi, n: (bi, 0, 0))),
        compiler_params=_params("parallel", "arbitrary"),
        name="moba_prep",
    )(proj3, proj3, proj3, q_gain.astype(F32).reshape(1, HEAD_DIM), k_gain.astype(F32).reshape(1, HEAD_DIM))


def _bucket_thresholds():
    max_exact = RP_BUCKETS // 2
    th = list(range(1, max_exact + 1))
    for kk in range(1, RP_BUCKETS - max_exact):
        x = max_exact * (RP_MAX_DIST / max_exact) ** (kk / (RP_BUCKETS - max_exact))
        r = round(x)
        th.append(r if abs(x - r) < 1e-9 else math.ceil(x))
    return th


N_NEAR = 5
LOG2E = 1.4426950408889634


def _moba_kernel(rb_ref, q_ref, k_ref, vt_ref, km_ref, o_ref, tab_ref, qa_ref, m_ref, alpha_ref, acc_ref, p_ref,
                 *, nb, hb):
    hg = pl.program_id(0)
    bi = pl.program_id(1)
    it = pl.program_id(2)
    blk = MB_BLOCK
    nq = MB_QBLOCKS
    tq = nq * blk
    shift = int(math.log2(blk))
    last = nq * it + nq - 1
    thresholds = _bucket_thresholds()
    heads = list(range(hb))
    hsl = lambda hh: slice(hh * HEAD_DIM, (hh + 1) * HEAD_DIM)

    @pl.when(jnp.logical_and(bi == 0, it == 0))
    def _():
        ko = lax.broadcasted_iota(jnp.int32, (blk, blk), 0)
        qo = lax.broadcasted_iota(jnp.int32, (blk, blk), 1)
        for hh in heads:
            h = hg * hb + hh
            far = rb_ref[RP_BUCKETS - 1, h]
            for delta in range(N_NEAR):
                d = delta * blk + qo - ko
                acc = jnp.full((blk, blk), rb_ref[0, h], F32)
                for bkt in range(1, RP_BUCKETS):
                    if delta * blk + blk - 1 >= thresholds[bkt - 1]:
                        acc = jnp.where(d >= thresholds[bkt - 1], rb_ref[bkt, h], acc)
                acc = (acc - far) * LOG2E
                if delta == 0:
                    acc = jnp.where(d >= 0, acc, NEG_BIG)
                tab_ref[hh, delta] = acc
            tab_ref[hh, N_NEAR] = jnp.zeros((blk, blk), F32)

    bidx = lax.broadcasted_iota(jnp.int32, (nb, tq), 0)
    qblk = nq * it + (lax.broadcasted_iota(jnp.int32, (nb, tq), 1) >> shift)
    past = bidx < qblk
    gates = []
    for hh in heads:
        kmh, kml = _split_bf16(km_ref[0, :, hsl(hh)])
        qb = q_ref[0, :, hsl(hh)]
        gate = (lax.dot_general(kmh, qb, NT_DIMS, preferred_element_type=F32)
                + lax.dot_general(kml, qb, NT_DIMS, preferred_element_type=F32))
        gates.append(jnp.where(past, gate, -jnp.inf))
    for hh, gate in zip(heads, gates):
        rank = jnp.zeros((nb, tq), jnp.int32)
        for jp in range(nb):
            gj = gate[jp:jp + 1, :]
            ahead = jnp.logical_or(gj > gate, jnp.logical_and(gj == gate, jp < bidx))
            rank = rank + ahead.astype(jnp.int32)
        allowed = jnp.logical_or(jnp.logical_and(past, rank < MB_TOPK), bidx == qblk)
        mask = jnp.where(allowed, 0.0, NEG_BIG)
        q_t = (q_ref[0, :, hsl(hh)].astype(F32) * LOG2E).T.astype(BF16)
        mask_rows = jnp.concatenate([mask, jnp.zeros((LANES - nb, tq), F32)], axis=0).astype(BF16)
        qa_ref[hh] = jnp.concatenate([q_t, mask_rows], axis=0)

    m_ref[...] = jnp.full(m_ref.shape, NEG_BIG, F32)
    acc_ref[...] = jnp.zeros(acc_ref.shape, F32)

    def logits(t, near, hs):
        j = last - t
        ss = [jnp.dot(k_ref[0, j, :, 2 * hh * HEAD_DIM:2 * (hh + 1) * HEAD_DIM], qa_ref[hh],
                      preferred_element_type=F32) for hh in hs]
        if near:
            idx = [jnp.clip(t - (nq - 1 - a), 0, N_NEAR) for a in range(nq)]
            ss = [jnp.concatenate([s[:, a * blk:(a + 1) * blk] + tab_ref[hh, idx[a]] for a in range(nq)], axis=1)
                  for hh, s in zip(hs, ss)]
        return ss

    def values(t, hs):
        slot = t & 1
        return [jnp.dot(vt_ref[0, last - t, hh], p_ref[slot, hh], preferred_element_type=F32) for hh in hs]

    def probabilities(t, ss, hs):
        slot = t & 1
        alphas = []
        for hh, s in zip(hs, ss):
            for q0 in range(0, tq, LANES):
                qs = slice(q0, q0 + LANES)
                m_old = m_ref[hh, :, qs]
                m_new = jnp.maximum(m_old, jnp.max(s[:, qs], axis=0, keepdims=True))
                m_ref[hh, :, qs] = m_new
                p_ref[slot, hh, :, qs] = jnp.exp2(s[:, qs] - m_new).astype(BF16)
                alphas.append((hh, qs, jnp.exp2(m_old - m_new)))
        return alphas

    def accumulate(pvs, hs):
        for hh, pv in zip(hs, pvs):
            acc_ref[hh] = alpha_ref[hh] * acc_ref[hh] + pv

    def set_alpha(alphas):
        for hh, qs, alpha in alphas:
            alpha_ref[hh, :, qs] = alpha

    groups = [heads[g:g + MB_ISSUE_GROUP] for g in range(0, hb, MB_ISSUE_GROUP)]

    def make_body(near):
        def body(t, carry):
            ahead = [logits(t, near, grp) for grp in groups[:MB_ISSUE_AHEAD]]
            for gi, grp in enumerate(groups):
                if gi + MB_ISSUE_AHEAD < len(groups):
                    ahead.append(logits(t, near, groups[gi + MB_ISSUE_AHEAD]))
                pvs = values(t - 1, grp)
                alphas = probabilities(t, ahead[gi], grp)
                accumulate(pvs, grp)
                set_alpha(alphas)
            return carry
        return body

    set_alpha(probabilities(0, logits(0, True, heads), heads))
    n_near = jnp.minimum(last + 1, N_NEAR + nq - 1)
    lax.fori_loop(1, n_near, make_body(True), 0)
    lax.fori_loop(n_near, last + 1, make_body(False), 0)
    accumulate(values(last, heads), heads)
    for hh in heads:
        acc = acc_ref[hh]
        o_ref[0, :, hsl(hh)] = (acc[:HEAD_DIM] / acc[HEAD_DIM:HEAD_DIM + 1]).T.astype(o_ref.dtype)


def _moba(qn, kn, vt, kmean, rel_bias):
    b, t, w = qn.shape
    nb = t // MB_BLOCK
    hb = MB_HEAD_BLOCK
    hw = hb * HEAD_DIM
    tq = MB_QBLOCKS * MB_BLOCK
    return pl.pallas_call(
        functools.partial(_moba_kernel, nb=nb, hb=hb),
        out_shape=jax.ShapeDtypeStruct((b, t, w), BF16),
        grid=(HEADS // hb, b, t // tq),
        in_specs=[pl.BlockSpec(memory_space=pltpu.SMEM),
                  pl.BlockSpec((1, tq, hw), lambda g, bi, i: (bi, i, g)),
                  pl.BlockSpec((1, nb, MB_BLOCK, 2 * hw), lambda g, bi, i: (bi, 0, 0, g),
                               pipeline_mode=pl.Buffered(1)),
                  pl.BlockSpec((1, nb, hb, MB_VROWS, MB_BLOCK), lambda g, bi, i: (bi, 0, g, 0, 0),
                               pipeline_mode=pl.Buffered(1)),
                  pl.BlockSpec((1, nb, hw), lambda g, bi, i: (bi, 0, g))],
        out_specs=pl.BlockSpec((1, tq, hw), lambda g, bi, i: (bi, i, g)),
        scratch_shapes=[pltpu.VMEM((hb, N_NEAR + 1, MB_BLOCK, MB_BLOCK), F32),
                        pltpu.VMEM((hb, 2 * HEAD_DIM, tq), BF16),
                        pltpu.VMEM((hb, 1, tq), F32),
                        pltpu.VMEM((hb, 1, tq), F32),
                        pltpu.VMEM((hb, MB_VROWS, tq), F32),
                        pltpu.VMEM((2, hb, MB_BLOCK, tq), BF16)],
        compiler_params=_params("arbitrary", "arbitrary", "arbitrary"),
        name="moba_attn",
    )(rel_bias.astype(F32), qn, kn, vt, kmean)


def _merge_kernel(ya_ref, yb_ref, wa_ref, wb_ref, ga_ref, gb_ref, o_ref):
    pa = jnp.dot(ya_ref[...], wa_ref[...], preferred_element_type=F32)
    pb = jnp.dot(yb_ref[...], wb_ref[...], preferred_element_type=F32)
    ga = ga_ref[...].astype(F32)
    gb = gb_ref[...].astype(F32)
    o_ref[...] = (jax.nn.sigmoid(ga) * pa + jax.nn.sigmoid(gb) * pb).astype(o_ref.dtype)


def _merge(ya, yb, wa, wb, proj, gate_col):
    m, d = ya.shape
    tm, tn = 1024, 512
    gpb = d // tn
    return pl.pallas_call(
        _merge_kernel,
        out_shape=jax.ShapeDtypeStruct((m, d), BF16),
        grid=(m // tm, d // tn),
        in_specs=[pl.BlockSpec((tm, d), lambda i, j: (i, 0)),
                  pl.BlockSpec((tm, d), lambda i, j: (i, 0)),
                  pl.BlockSpec((d, tn), lambda i, j: (0, j)),
                  pl.BlockSpec((d, tn), lambda i, j: (0, j)),
                  pl.BlockSpec((tm, tn), lambda i, j: (i, gate_col * gpb + j)),
                  pl.BlockSpec((tm, tn), lambda i, j: (i, (gate_col + 1) * gpb + j))],
        out_specs=pl.BlockSpec((tm, tn), lambda i, j: (i, j)),
        compiler_params=_params("parallel", "arbitrary"),
        name="merge",
    )(ya, yb, wa, wb, proj, proj)


def _outproj_kernel(x_ref, w_ref, h_ref, ada_ref, o_ref):
    o_ref[...] = h_ref[...] + ada_ref[0] * jnp.dot(x_ref[...], w_ref[...], preferred_element_type=F32)


def _outproj(x, w, h, gate3, *, seq):
    m, d = x.shape
    tm, tn = 1024, 512
    per_batch = seq // tm
    return pl.pallas_call(
        _outproj_kernel,
        out_shape=jax.ShapeDtypeStruct((m, d), F32),
        grid=(m // tm, d // tn),
        in_specs=[pl.BlockSpec((tm, d), lambda i, j: (i, 0)),
                  pl.BlockSpec((d, tn), lambda i, j: (0, j)),
                  pl.BlockSpec((tm, tn), lambda i, j: (i, j)),
                  pl.BlockSpec((1, 1, tn), lambda i, j: (i // per_batch, 0, j))],
        out_specs=pl.BlockSpec((tm, tn), lambda i, j: (i, j)),
        compiler_params=_params("parallel", "arbitrary"),
        name="out_proj",
    )(x, w, h, gate3)


def kernel(x, c, ada_w, ada_b, norm1_g, ffn1_w1, ffn1_w3, ffn1_w2, norm2_g, w_in, dn_conv_w, dn_a_log, dn_dt_bias, dn_norm_g, mb_q_norm_g, mb_k_norm_g, rel_bias, w_proj_a, w_proj_b, w_out, norm3_g, ffn2_w1, ffn2_w3, ffn2_w2):
    b, t, d = x.shape
    m = b * t
    depth = ada_w.shape[0]
    wide = HEADS * HEAD_DIM
    cut0 = 4 * wide
    cut1 = cut0 + 2 * HEADS
    h = x.reshape(m, d)
    c_pad = jnp.zeros((8, d), F32).at[:b].set(c)
    for l in range(depth):
        ada = _ada(c_pad, ada_w, ada_b.reshape(depth, 1, -1), l)[:b]
        ada3 = ada.reshape(b, N_ADA, d)
        h = _ffn(h, ada3, norm1_g[l].reshape(1, d), ffn1_w1[l].astype(BF16), ffn1_w3[l].astype(BF16),
                 ffn1_w2[l].astype(BF16), slot=0, seq=t)

        w_in16 = w_in.astype(BF16)
        w_tail = w_in16[l][:, cut1:]
        w_ba = jnp.zeros((d, LANES), BF16).at[:, :2 * HEADS].set(w_in16[l][:, cut0:cut1])
        proj, ba = _inproj(h, ada3, norm2_g[l].reshape(1, d), w_in16, l, cut0, w_tail, w_ba, seq=t)
        proj3 = proj.reshape(b, t, -1)
        ba3 = ba.reshape(b, t, LANES)
        bat3 = jnp.transpose(ba3, (0, 2, 1))
        y_a = _deltanet(proj3, ba3, bat3, dn_conv_w[l], dn_a_log[l], dn_dt_bias[l], dn_norm_g[l])
        qn, kn, vt, kmean = _moba_prep(proj3, mb_q_norm_g[l], mb_k_norm_g[l])
        y_b = _moba(qn, kn, vt, kmean, rel_bias)
        merged = _merge(y_a.reshape(m, wide), y_b.reshape(m, wide), w_proj_a[l].astype(BF16),
                        w_proj_b[l].astype(BF16), proj, 7)
        gate2 = ada3[:, 5:6, :]
        h = _outproj(merged, w_out[l].astype(BF16), h, gate2, seq=t)

        h = _ffn(h, ada3, norm3_g[l].reshape(1, d), ffn2_w1[l].astype(BF16), ffn2_w3[l].astype(BF16),
                 ffn2_w2[l].astype(BF16), slot=6, seq=t)
    return h.reshape(b, t, d)
```

```python
import functools
import math

import jax
import jax.numpy as jnp
from jax import lax
from jax.experimental import pallas as pl
from jax.experimental.pallas import tpu as pltpu

F32 = jnp.float32
BF16 = jnp.bfloat16

LANES = 128
SUBLANES = 8
NORM_EPS = 1e-6
N_ADA = 9
HEADS = 16
HEAD_DIM = 128
DN_CONV = 4
DN_CHUNK = 128
DN_HEAD_BLOCK = 16
DN_INV_BASE = 8
MB_BLOCK = 256
MB_TOPK = 3
MB_HEAD_BLOCK = 8
MB_QBLOCKS = 1
MB_ISSUE_GROUP = 1
MB_ISSUE_AHEAD = 2
MB_VROWS = HEAD_DIM + 16
RP_BUCKETS = 32
RP_MAX_DIST = 1024
NEG_BIG = -1e30
VMEM_LIMIT = 56 * 1024 * 1024

NT_DIMS = (((1,), (1,)), ((), ()))
NN_DIMS = (((1,), (0,)), ((), ()))
TN_DIMS = (((0,), (0,)), ((), ()))


def _dot(a, b, dims=NN_DIMS):
    return lax.dot_general(a.astype(BF16), b.astype(BF16), dims, preferred_element_type=F32)


def _split_bf16(x):
    hi = x.astype(BF16)
    lo = (x - hi.astype(F32)).astype(BF16)
    return hi, lo


def _dot3(a, b, dims=NN_DIMS):
    ah, al = _split_bf16(a)
    bh, bl = _split_bf16(b)
    f = lambda x, y: lax.dot_general(x, y, dims, preferred_element_type=F32)
    return f(ah, bh) + (f(ah, bl) + f(al, bh))


def _silu(x):
    return x * jax.nn.sigmoid(x)


def _params(*sem):
    return pltpu.CompilerParams(dimension_semantics=sem, vmem_limit_bytes=VMEM_LIMIT)


def _ada_kernel(c_ref, w_ref, b_ref, o_ref):
    s = _silu(c_ref[...])
    o_ref[...] = _dot3(s, w_ref[0]) + b_ref[0]


def _ada(c_pad, w, b, layer):
    rows, d = c_pad.shape
    n = w.shape[2]
    tn = 1024
    return pl.pallas_call(
        _ada_kernel,
        out_shape=jax.ShapeDtypeStruct((rows, n), F32),
        grid=(n // tn,),
        in_specs=[pl.BlockSpec((rows, d), lambda j: (0, 0)),
                  pl.BlockSpec((1, d, tn), lambda j: (layer, 0, j)),
                  pl.BlockSpec((1, 1, tn), lambda j: (layer, 0, j))],
        out_specs=pl.BlockSpec((rows, tn), lambda j: (0, j)),
        compiler_params=_params("arbitrary"),
        name="ada_proj",
    )(c_pad, w, b)


def _norm_modulate(h, gain, shift, scale):
    ms = jnp.mean(h * h, axis=-1, keepdims=True)
    return (h * lax.rsqrt(ms + NORM_EPS)) * (gain * (1.0 + scale)) + shift


def _ffn_kernel(h_ref, ada_ref, g_ref, w1_ref, w3_ref, w2_ref, o_ref, u_ref, acc_ref, act_ref, *, slot, nf):
    f = pl.program_id(1)

    def up(dst):
        u = u_ref[...]
        a = jnp.dot(u, w1_ref[...], preferred_element_type=F32)
        b = jnp.dot(u, w3_ref[...], preferred_element_type=F32)
        act_ref[dst] = (_silu(a) * b).astype(BF16)

    def down(src):
        acc_ref[...] += jnp.dot(act_ref[src], w2_ref[...], preferred_element_type=F32)

    @pl.when(f == 0)
    def _():
        u = _norm_modulate(h_ref[...], g_ref[...], ada_ref[0, slot:slot + 1, :], ada_ref[0, slot + 1:slot + 2, :])
        u_ref[...] = u.astype(BF16)
        acc_ref[...] = jnp.zeros_like(acc_ref)
        up(0)

    @pl.when(jnp.logical_and(f > 0, f < nf))
    def _():
        down(1 - (f & 1))
        up(f & 1)

    @pl.when(f == nf)
    def _():
        down((nf - 1) & 1)
        o_ref[...] = h_ref[...] + 0.5 * ada_ref[0, slot + 2:slot + 3, :] * acc_ref[...]


def _ffn(h, ada3, gain, w1, w3, w2, *, slot, seq):
    m, d = h.shape
    dff = w1.shape[1]
    tm, tf = 512, 512
    nf = dff // tf
    per_batch = seq // tm
    return pl.pallas_call(
        functools.partial(_ffn_kernel, slot=slot, nf=nf),
        out_shape=jax.ShapeDtypeStruct((m, d), F32),
        grid=(m // tm, nf + 1),
        in_specs=[pl.BlockSpec((tm, d), lambda i, f: (i, 0)),
                  pl.BlockSpec((1, N_ADA, d), lambda i, f: (i // per_batch, 0, 0)),
                  pl.BlockSpec((1, d), lambda i, f: (0, 0)),
                  pl.BlockSpec((d, tf), lambda i, f: (0, jnp.minimum(f, nf - 1))),
                  pl.BlockSpec((d, tf), lambda i, f: (0, jnp.minimum(f, nf - 1))),
                  pl.BlockSpec((tf, d), lambda i, f: (jnp.maximum(f - 1, 0), 0))],
        out_specs=pl.BlockSpec((tm, d), lambda i, f: (i, 0)),
        scratch_shapes=[pltpu.VMEM((tm, d), BF16), pltpu.VMEM((tm, d), F32), pltpu.VMEM((2, tm, tf), BF16)],
        compiler_params=_params("parallel", "arbitrary"),
        name="ffn",
    )(h, ada3, gain, w1, w3, w2)


def _inproj_kernel(h_ref, ada_ref, g_ref, wa_ref, wb_ref, wba_ref, o_ref, ba_ref, u_ref, *, na):
    j = pl.program_id(1)

    @pl.when(j == 0)
    def _():
        u = _norm_modulate(h_ref[...], g_ref[...], ada_ref[0, 3:4, :], ada_ref[0, 4:5, :])
        u_ref[...] = u.astype(BF16)
        ba_ref[...] = jnp.dot(u_ref[...], wba_ref[...], preferred_element_type=F32)

    @pl.when(j < na)
    def _():
        o_ref[...] = jnp.dot(u_ref[...], wa_ref[0], preferred_element_type=F32).astype(o_ref.dtype)

    @pl.when(j >= na)
    def _():
        o_ref[...] = jnp.dot(u_ref[...], wb_ref[...], preferred_element_type=F32).astype(o_ref.dtype)


def _inproj(h, ada3, gain, w_all, layer, n_lead, wb, wba, *, seq):
    m, d = h.shape
    tm, tn = 1024, 1024
    na = n_lead // tn
    n = n_lead + wb.shape[1]
    per_batch = seq // tm
    return pl.pallas_call(
        functools.partial(_inproj_kernel, na=na),
        out_shape=(jax.ShapeDtypeStruct((m, n), BF16), jax.ShapeDtypeStruct((m, LANES), F32)),
        grid=(m // tm, n // tn),
        in_specs=[pl.BlockSpec((tm, d), lambda i, j: (i, 0)),
                  pl.BlockSpec((1, N_ADA, d), lambda i, j: (i // per_batch, 0, 0)),
                  pl.BlockSpec((1, d), lambda i, j: (0, 0)),
                  pl.BlockSpec((1, d, tn), lambda i, j: (layer, 0, jnp.minimum(j, na - 1))),
                  pl.BlockSpec((d, tn), lambda i, j: (0, jnp.maximum(j - na, 0))),
                  pl.BlockSpec((d, LANES), lambda i, j: (0, 0))],
        out_specs=(pl.BlockSpec((tm, tn), lambda i, j: (i, j)),
                   pl.BlockSpec((tm, LANES), lambda i, j: (i, 0))),
        scratch_shapes=[pltpu.VMEM((tm, d), BF16)],
        compiler_params=_params("parallel", "arbitrary"),
        name="in_proj",
    )(h, ada3, gain, w_all, wb, wba)


def _softplus(x):
    return jnp.maximum(x, 0.0) + jnp.log1p(jnp.exp(-jnp.abs(x)))


def _dn_kernel(q_ref, k_ref, v_ref, z_ref, ba_ref, bat_ref, wq_ref, wk_ref, wv_ref,
               arow_ref, dtrow_ref, acol_ref, dtcol_ref, gain_ref, o_ref,
               prev_ref, state_ref, gcrow_ref, *, hb):
    c = DN_CHUNK
    hg = pl.program_id(1)
    n = pl.program_id(2)

    @pl.when(n == 0)
    def _():
        prev_ref[...] = jnp.zeros_like(prev_ref)
        state_ref[...] = jnp.zeros_like(state_ref)

    width = hb * HEAD_DIM
    sub = prev_ref.shape[1]
    row_t = lax.broadcasted_iota(jnp.int32, (sub, width), 0)

    def conv_silu(x_ref, w_ref, s):
        cur = x_ref[0].astype(F32)
        tail = prev_ref[s]
        acc = cur * w_ref[DN_CONV - 1:DN_CONV, :]
        for d in range(1, DN_CONV):
            shifted = pltpu.roll(cur, d, 0)
            head = jnp.where(row_t < d, pltpu.roll(tail, d, 0), shifted[:sub])
            shifted = jnp.concatenate([head, shifted[sub:]], axis=0)
            acc = acc + shifted * w_ref[DN_CONV - 1 - d:DN_CONV - d, :]
        prev_ref[s] = cur[c - sub:]
        return _silu(acc)

    xq = conv_silu(q_ref, wq_ref, 0)
    xk = conv_silu(k_ref, wk_ref, 1)
    xv = conv_silu(v_ref, wv_ref, 2)

    slab = ba_ref[0]
    beta_mat = jax.nn.sigmoid(slab)
    g_mat = -jnp.exp(arow_ref[...]) * _softplus(slab + dtrow_ref[...])
    ri = lax.broadcasted_iota(jnp.int32, (c, c), 0)
    ci = lax.broadcasted_iota(jnp.int32, (c, c), 1)
    lower_incl = (ri >= ci)
    tri_l = jnp.where(lower_incl, 1.0, 0.0).astype(BF16)
    tri_u = jnp.where(ri <= ci, 1.0, 0.0).astype(BF16)

    def exact_ones_dot(x, ones, left):
        x1 = x.astype(BF16)
        r1 = x - x1.astype(F32)
        x2 = r1.astype(BF16)
        x3 = (r1 - x2.astype(F32)).astype(BF16)
        if left:
            f = lambda p: jnp.dot(ones, p, preferred_element_type=F32)
        else:
            f = lambda p: jnp.dot(p, ones, preferred_element_type=F32)
        return f(x1) + (f(x2) + f(x3))

    gc_colmat = exact_ones_dot(g_mat, tri_l, True)
    a_rows = bat_ref[0, HEADS:2 * HEADS, :]
    g_rows = -jnp.exp(acol_ref[...]) * _softplus(a_rows + dtcol_ref[...])
    gcrow_ref[...] = exact_ones_dot(g_rows, tri_u, False)

    lane = lax.broadcasted_iota(jnp.int32, (c, LANES), 1)
    eye = jnp.where(ri == ci, 1.0, 0.0)
    strict = ri > ci
    scale = HEAD_DIM ** -0.5

    hs = []
    for hh in range(hb):
        h = hg * hb + hh
        sl = slice(hh * HEAD_DIM, (hh + 1) * HEAD_DIM)
        beta = jnp.sum(jnp.where(lane == h, beta_mat, 0.0), axis=1, keepdims=True)
        gcc = jnp.sum(jnp.where(lane == HEADS + h, gc_colmat, 0.0), axis=1, keepdims=True)
        gcr = gcrow_ref[pl.ds(h, 1), :]
        gc_last = gcr[:, c - 1:c]
        decay = jnp.where(lower_incl, jnp.exp(jnp.minimum(gcc - gcr, 0.0)), 0.0)
        q = xq[:, sl]
        k = xk[:, sl]
        q = q * (lax.rsqrt(jnp.sum(q * q, axis=-1, keepdims=True) + NORM_EPS) * scale)
        k = k * lax.rsqrt(jnp.sum(k * k, axis=-1, keepdims=True) + NORM_EPS)
        kb = k * beta
        egc = jnp.exp(gcc)
        hs.append(dict(hh=hh, sl=sl, decay=decay, q=q, k=k, kb=kb,
                       rhs=jnp.concatenate([xv[:, sl] * beta, kb * egc], axis=1),
                       qd=q * egc, kd=k * jnp.exp(gc_last - gcc), g_tot=jnp.exp(gc_last)))

    for s in hs:
        s["lmat"] = jnp.where(strict, _dot(s["kb"], s["k"], NT_DIMS) * s["decay"], 0.0)
        s["qk"] = _dot(s["q"], s["k"], NT_DIMS) * s["decay"]
    shift0 = int(math.log2(DN_INV_BASE))
    mm = lambda a, b: jnp.dot(a, b, preferred_element_type=F32)
    same_block = jnp.where((ri >> shift0) == (ci >> shift0), 1.0, 0.0).astype(BF16)
    for s in hs:
        s["lb"] = s["lmat"].astype(BF16)
        s["pw"] = -(s["lb"] * same_block)
        s["tmat"] = eye + s["pw"].astype(F32)
    for _ in range(shift0 - 1):
        for s in hs:
            s["pw"] = mm(s["pw"], s["pw"]).astype(BF16)
        for s in hs:
            s["tmat"] = s["tmat"] + mm(s["tmat"].astype(BF16), s["pw"])
    for sh in range(shift0, int(math.log2(c))):
        lower_left = jnp.where(jnp.logical_and((ri >> sh) == (ci >> sh) + 1, ((ci >> sh) & 1) == 0),
                               1.0, 0.0).astype(BF16)
        for s in hs:
            s["tb"] = s["tmat"].astype(BF16)
            s["pw"] = mm(s["lb"] * lower_left, s["tb"]).astype(BF16)
        for s in hs:
            s["tmat"] = s["tmat"] - mm(s["tb"], s["pw"])
    for s in hs:
        s["uw"] = _dot(s["tmat"], s["rhs"])
    for s in hs:
        s["state"] = state_ref[s["hh"]]
        s["v_new"] = s["uw"][:, :HEAD_DIM] - _dot(s["uw"][:, HEAD_DIM:], s["state"])
    for s in hs:
        s["o"] = _dot(s["qd"], s["state"]) + _dot(s["qk"], s["v_new"])
        state_ref[s["hh"]] = s["state"] * s["g_tot"] + _dot(s["kd"], s["v_new"], TN_DIMS)
    for s in hs:
        o = s["o"]
        o = o * lax.rsqrt(jnp.mean(o * o, axis=-1, keepdims=True) + NORM_EPS) * gain_ref[...]
        o_ref[0, :, s["sl"]] = (o * _silu(z_ref[0, :, s["sl"]].astype(F32))).astype(o_ref.dtype)


def _deltanet(proj3, ba3, bat3, conv_w, a_log, dt_bias, gain):
    b, t, _ = proj3.shape
    hb = DN_HEAD_BLOCK
    c = DN_CHUNK
    width = hb * HEAD_DIM
    ng = HEADS // hb
    region = (HEADS * HEAD_DIM) // width

    def pad_row(x):
        return jnp.zeros((1, LANES), F32).at[0, HEADS:2 * HEADS].set(x.astype(F32))

    arow, dtrow = pad_row(a_log), pad_row(dt_bias)
    acol = a_log.astype(F32).reshape(HEADS, 1)
    dtcol = dt_bias.astype(F32).reshape(HEADS, 1)
    gain2 = gain.astype(F32).reshape(1, HEAD_DIM)

    def col(r):
        return pl.BlockSpec((1, c, width), lambda bi, g, n, r=r: (bi, n, r * region + g))

    def wcol(r):
        return pl.BlockSpec((DN_CONV, width), lambda bi, g, n, r=r: (0, r * region + g))

    const = lambda shape: pl.BlockSpec(shape, lambda bi, g, n: tuple(0 for _ in shape))
    return pl.pallas_call(
        functools.partial(_dn_kernel, hb=hb),
        out_shape=jax.ShapeDtypeStruct((b, t, HEADS * HEAD_DIM), BF16),
        grid=(b, ng, t // c),
        in_specs=[col(0), col(1), col(2), col(3),
                  pl.BlockSpec((1, c, LANES), lambda bi, g, n: (bi, n, 0)),
                  pl.BlockSpec((1, LANES, c), lambda bi, g, n: (bi, 0, n)),
                  wcol(0), wcol(1), wcol(2),
                  const((1, LANES)), const((1, LANES)), const((HEADS, 1)), const((HEADS, 1)),
                  const((1, HEAD_DIM))],
        out_specs=pl.BlockSpec((1, c, width), lambda bi, g, n: (bi, n, g)),
        scratch_shapes=[pltpu.VMEM((3, SUBLANES, width), F32),
                        pltpu.VMEM((hb, HEAD_DIM, HEAD_DIM), F32),
                        pltpu.VMEM((HEADS, c), F32)],
        compiler_params=_params("parallel", "parallel", "arbitrary"),
        name="deltanet",
    )(proj3, proj3, proj3, proj3, ba3, bat3, conv_w, conv_w, conv_w, arow, dtrow, acol, dtcol, gain2)


def _moba_prep_kernel(q_ref, k_ref, v_ref, qg_ref, kg_ref, qn_ref, kn_ref, vt_ref, km_ref):
    n = pl.program_id(1)
    scale = HEAD_DIM ** -0.5

    @pl.when(n == 0)
    def _():
        km_ref[...] = jnp.zeros_like(km_ref)

    row = lax.broadcasted_iota(jnp.int32, (km_ref.shape[1], HEAD_DIM), 0)
    ones = jnp.ones((MB_VROWS - HEAD_DIM, MB_BLOCK), BF16)
    lane = lax.broadcasted_iota(jnp.int32, (MB_BLOCK, LANES), 1)
    onehot = jnp.where(lane == n, 1.0, 0.0).astype(BF16)
    for h in range(HEADS):
        sl = slice(h * HEAD_DIM, (h + 1) * HEAD_DIM)
        ka = 2 * h * HEAD_DIM
        q = q_ref[0, :, sl].astype(F32)
        k = k_ref[0, :, sl].astype(F32)
        qn = q * lax.rsqrt(jnp.mean(q * q, axis=-1, keepdims=True) + NORM_EPS) * qg_ref[...]
        kn = k * lax.rsqrt(jnp.mean(k * k, axis=-1, keepdims=True) + NORM_EPS) * kg_ref[...]
        qn_ref[0, :, sl] = (qn * scale).astype(qn_ref.dtype)
        kn_ref[0, 0, :, ka:ka + HEAD_DIM] = kn.astype(BF16)
        kn_ref[0, 0, :, ka + HEAD_DIM:ka + 2 * HEAD_DIM] = onehot
        km_ref[0, :, sl] = jnp.where(row == n, jnp.mean(kn, axis=0, keepdims=True), km_ref[0, :, sl])
        vt_ref[0, 0, h, :HEAD_DIM, :] = v_ref[0, :, sl].astype(F32).T.astype(BF16)
        vt_ref[0, 0, h, HEAD_DIM:, :] = ones


def _moba_prep(proj3, q_gain, k_gain):
    b, t, _ = proj3.shape
    nb = t // MB_BLOCK
    w = HEADS * HEAD_DIM
    col = lambda r: pl.BlockSpec((1, MB_BLOCK, w), lambda bi, n, r=r: (bi, n, r))
    gspec = pl.BlockSpec((1, HEAD_DIM), lambda bi, n: (0, 0))
    return pl.pallas_call(
        _moba_prep_kernel,
        out_shape=(jax.ShapeDtypeStruct((b, t, w), BF16),
                   jax.ShapeDtypeStruct((b, nb, MB_BLOCK, 2 * w), BF16),
                   jax.ShapeDtypeStruct((b, nb, HEADS, MB_VROWS, MB_BLOCK), BF16),
                   jax.ShapeDtypeStruct((b, nb, w), F32)),
        grid=(b, nb),
        in_specs=[col(4), col(5), col(6), gspec, gspec],
        out_specs=(pl.BlockSpec((1, MB_BLOCK, w), lambda bi, n: (bi, n, 0)),
                   pl.BlockSpec((1, 1, MB_BLOCK, 2 * w), lambda bi, n: (bi, n, 0, 0)),
                   pl.BlockSpec((1, 1, HEADS, MB_VROWS, MB_BLOCK), lambda bi, n: (bi, n, 0, 0, 0)),
                   pl.BlockSpec((1, nb, w), lambda bi, n: (bi, 0, 0))),
        compiler_params=_params("parallel", "arbitrary"),
        name="moba_prep",
    )(proj3, proj3, proj3, q_gain.astype(F32).reshape(1, HEAD_DIM), k_gain.astype(F32).reshape(1, HEAD_DIM))


def _bucket_thresholds():
    max_exact = RP_BUCKETS // 2
    th = list(range(1, max_exact + 1))
    for kk in range(1, RP_BUCKETS - max_exact):
        x = max_exact * (RP_MAX_DIST / max_exact) ** (kk / (RP_BUCKETS - max_exact))
        r = round(x)
        th.append(r if abs(x - r) < 1e-9 else math.ceil(x))
    return th


N_NEAR = 5
LOG2E = 1.4426950408889634


def _moba_kernel(rb_ref, q_ref, k_ref, vt_ref, km_ref, o_ref, tab_ref, qa_ref, m_ref, alpha_ref, acc_ref, p_ref,
                 *, nb, hb):
    hg = pl.program_id(0)
    bi = pl.program_id(1)
    it = pl.program_id(2)
    blk = MB_BLOCK
    nq = MB_QBLOCKS
    tq = nq * blk
    shift = int(math.log2(blk))
    last = nq * it + nq - 1
    thresholds = _bucket_thresholds()
    heads = list(range(hb))
    hsl = lambda hh: slice(hh * HEAD_DIM, (hh + 1) * HEAD_DIM)

    @pl.when(jnp.logical_and(bi == 0, it == 0))
    def _():
        ko = lax.broadcasted_iota(jnp.int32, (blk, blk), 0)
        qo = lax.broadcasted_iota(jnp.int32, (blk, blk), 1)
        for hh in heads:
            h = hg * hb + hh
            far = rb_ref[RP_BUCKETS - 1, h]
            for delta in range(N_NEAR):
                d = delta * blk + qo - ko
                acc = jnp.full((blk, blk), rb_ref[0, h], F32)
                for bkt in range(1, RP_BUCKETS):
                    if delta * blk + blk - 1 >= thresholds[bkt - 1]:
                        acc = jnp.where(d >= thresholds[bkt - 1], rb_ref[bkt, h], acc)
                acc = (acc - far) * LOG2E
                if delta == 0:
                    acc = jnp.where(d >= 0, acc, NEG_BIG)
                tab_ref[hh, delta] = acc
            tab_ref[hh, N_NEAR] = jnp.zeros((blk, blk), F32)

    bidx = lax.broadcasted_iota(jnp.int32, (nb, tq), 0)
    qblk = nq * it + (lax.broadcasted_iota(jnp.int32, (nb, tq), 1) >> shift)
    past = bidx < qblk
    gates = []
    for hh in heads:
        kmh, kml = _split_bf16(km_ref[0, :, hsl(hh)])
        qb = q_ref[0, :, hsl(hh)]
        gate = (lax.dot_general(kmh, qb, NT_DIMS, preferred_element_type=F32)
                + lax.dot_general(kml, qb, NT_DIMS, preferred_element_type=F32))
        gates.append(jnp.where(past, gate, -jnp.inf))
    for hh, gate in zip(heads, gates):
        rank = jnp.zeros((nb, tq), jnp.int32)
        for jp in range(nb):
            gj = gate[jp:jp + 1, :]
            ahead = jnp.logical_or(gj > gate, jnp.logical_and(gj == gate, jp < bidx))
            rank = rank + ahead.astype(jnp.int32)
        allowed = jnp.logical_or(jnp.logical_and(past, rank < MB_TOPK), bidx == qblk)
        mask = jnp.where(allowed, 0.0, NEG_BIG)
        q_t = (q_ref[0, :, hsl(hh)].astype(F32) * LOG2E).T.astype(BF16)
        mask_rows = jnp.concatenate([mask, jnp.zeros((LANES - nb, tq), F32)], axis=0).astype(BF16)
        qa_ref[hh] = jnp.concatenate([q_t, mask_rows], axis=0)

    m_ref[...] = jnp.full(m_ref.shape, NEG_BIG, F32)
    acc_ref[...] = jnp.zeros(acc_ref.shape, F32)

    def logits(t, near, hs):
        j = last - t
        ss = [jnp.dot(k_ref[0, j, :, 2 * hh * HEAD_DIM:2 * (hh + 1) * HEAD_DIM], qa_ref[hh],
                      preferred_element_type=F32) for hh in hs]
        if near:
            idx = [jnp.clip(t - (nq - 1 - a), 0, N_NEAR) for a in range(nq)]
            ss = [jnp.concatenate([s[:, a * blk:(a + 1) * blk] + tab_ref[hh, idx[a]] for a in range(nq)], axis=1)
                  for hh, s in zip(hs, ss)]
        return ss

    def values(t, hs):
        slot = t & 1
        return [jnp.dot(vt_ref[0, last - t, hh], p_ref[slot, hh], preferred_element_type=F32) for hh in hs]

    def probabilities(t, ss, hs):
        slot = t & 1
        alphas = []
        for hh, s in zip(hs, ss):
            for q0 in range(0, tq, LANES):
                qs = slice(q0, q0 + LANES)
                m_old = m_ref[hh, :, qs]
                m_new = jnp.maximum(m_old, jnp.max(s[:, qs], axis=0, keepdims=True))
                m_ref[hh, :, qs] = m_new
                p_ref[slot, hh, :, qs] = jnp.exp2(s[:, qs] - m_new).astype(BF16)
                alphas.append((hh, qs, jnp.exp2(m_old - m_new)))
        return alphas

    def accumulate(pvs, hs):
        for hh, pv in zip(hs, pvs):
            acc_ref[hh] = alpha_ref[hh] * acc_ref[hh] + pv

    def set_alpha(alphas):
        for hh, qs, alpha in alphas:
            alpha_ref[hh, :, qs] = alpha

    groups = [heads[g:g + MB_ISSUE_GROUP] for g in range(0, hb, MB_ISSUE_GROUP)]

    def make_body(near):
        def body(t, carry):
            ahead = [logits(t, near, grp) for grp in groups[:MB_ISSUE_AHEAD]]
            for gi, grp in enumerate(groups):
                if gi + MB_ISSUE_AHEAD < len(groups):
                    ahead.append(logits(t, near, groups[gi + MB_ISSUE_AHEAD]))
                pvs = values(t - 1, grp)
                alphas = probabilities(t, ahead[gi], grp)
                accumulate(pvs, grp)
                set_alpha(alphas)
            return carry
        return body

    set_alpha(probabilities(0, logits(0, True, heads), heads))
    n_near = jnp.minimum(last + 1, N_NEAR + nq - 1)
    lax.fori_loop(1, n_near, make_body(True), 0)
    lax.fori_loop(n_near, last + 1, make_body(False), 0)
    accumulate(values(last, heads), heads)
    for hh in heads:
        acc = acc_ref[hh]
        o_ref[0, :, hsl(hh)] = (acc[:HEAD_DIM] / acc[HEAD_DIM:HEAD_DIM + 1]).T.astype(o_ref.dtype)


def _moba(qn, kn, vt, kmean, rel_bias):
    b, t, w = qn.shape
    nb = t // MB_BLOCK
    hb = MB_HEAD_BLOCK
    hw = hb * HEAD_DIM
    tq = MB_QBLOCKS * MB_BLOCK
    return pl.pallas_call(
        functools.partial(_moba_kernel, nb=nb, hb=hb),
        out_shape=jax.ShapeDtypeStruct((b, t, w), BF16),
        grid=(HEADS // hb, b, t // tq),
        in_specs=[pl.BlockSpec(memory_space=pltpu.SMEM),
                  pl.BlockSpec((1, tq, hw), lambda g, bi, i: (bi, i, g)),
                  pl.BlockSpec((1, nb, MB_BLOCK, 2 * hw), lambda g, bi, i: (bi, 0, 0, g),
                               pipeline_mode=pl.Buffered(1)),
                  pl.BlockSpec((1, nb, hb, MB_VROWS, MB_BLOCK), lambda g, bi, i: (bi, 0, g, 0, 0),
                               pipeline_mode=pl.Buffered(1)),
                  pl.BlockSpec((1, nb, hw), lambda g, bi, i: (bi, 0, g))],
        out_specs=pl.BlockSpec((1, tq, hw), lambda g, bi, i: (bi, i, g)),
        scratch_shapes=[pltpu.VMEM((hb, N_NEAR + 1, MB_BLOCK, MB_BLOCK), F32),
                        pltpu.VMEM((hb, 2 * HEAD_DIM, tq), BF16),
                        pltpu.VMEM((hb, 1, tq), F32),
                        pltpu.VMEM((hb, 1, tq), F32),
                        pltpu.VMEM((hb, MB_VROWS, tq), F32),
                        pltpu.VMEM((2, hb, MB_BLOCK, tq), BF16)],
        compiler_params=_params("arbitrary", "arbitrary", "arbitrary"),
        name="moba_attn",
    )(rel_bias.astype(F32), qn, kn, vt, kmean)


def _merge_kernel(ya_ref, yb_ref, wa_ref, wb_ref, ga_ref, gb_ref, o_ref):
    pa = jnp.dot(ya_ref[...], wa_ref[...], preferred_element_type=F32)
    pb = jnp.dot(yb_ref[...], wb_ref[...], preferred_element_type=F32)
    ga = ga_ref[...].astype(F32)
    gb = gb_ref[...].astype(F32)
    o_ref[...] = (jax.nn.sigmoid(ga) * pa + jax.nn.sigmoid(gb) * pb).astype(o_ref.dtype)


def _merge(ya, yb, wa, wb, proj, gate_col):
    m, d = ya.shape
    tm, tn = 1024, 512
    gpb = d // tn
    return pl.pallas_call(
        _merge_kernel,
        out_shape=jax.ShapeDtypeStruct((m, d), BF16),
        grid=(m // tm, d // tn),
        in_specs=[pl.BlockSpec((tm, d), lambda i, j: (i, 0)),
                  pl.BlockSpec((tm, d), lambda i, j: (i, 0)),
                  pl.BlockSpec((d, tn), lambda i, j: (0, j)),
                  pl.BlockSpec((d, tn), lambda i, j: (0, j)),
                  pl.BlockSpec((tm, tn), lambda i, j: (i, gate_col * gpb + j)),
                  pl.BlockSpec((tm, tn), lambda i, j: (i, (gate_col + 1) * gpb + j))],
        out_specs=pl.BlockSpec((tm, tn), lambda i, j: (i, j)),
        compiler_params=_params("parallel", "arbitrary"),
        name="merge",
    )(ya, yb, wa, wb, proj, proj)


def _outproj_kernel(x_ref, w_ref, h_ref, ada_ref, o_ref):
    o_ref[...] = h_ref[...] + ada_ref[0] * jnp.dot(x_ref[...], w_ref[...], preferred_element_type=F32)


def _outproj(x, w, h, gate3, *, seq):
    m, d = x.shape
    tm, tn = 1024, 512
    per_batch = seq // tm
    return pl.pallas_call(
        _outproj_kernel,
        out_shape=jax.ShapeDtypeStruct((m, d), F32),
        grid=(m // tm, d // tn),
        in_specs=[pl.BlockSpec((tm, d), lambda i, j: (i, 0)),
                  pl.BlockSpec((d, tn), lambda i, j: (0, j)),
                  pl.BlockSpec((tm, tn), lambda i, j: (i, j)),
                  pl.BlockSpec((1, 1, tn), lambda i, j: (i // per_batch, 0, j))],
        out_specs=pl.BlockSpec((tm, tn), lambda i, j: (i, j)),
        compiler_params=_params("parallel", "arbitrary"),
        name="out_proj",
    )(x, w, h, gate3)


def kernel(x, c, ada_w, ada_b, norm1_g, ffn1_w1, ffn1_w3, ffn1_w2, norm2_g, w_in, dn_conv_w, dn_a_log, dn_dt_bias, dn_norm_g, mb_q_norm_g, mb_k_norm_g, rel_bias, w_proj_a, w_proj_b, w_out, norm3_g, ffn2_w1, ffn2_w3, ffn2_w2):
    b, t, d = x.shape
    m = b * t
    depth = ada_w.shape[0]
    wide = HEADS * HEAD_DIM
    cut0 = 4 * wide
    cut1 = cut0 + 2 * HEADS
    h = x.reshape(m, d)
    c_pad = jnp.zeros((8, d), F32).at[:b].set(c)
    for l in range(depth):
        ada = _ada(c_pad, ada_w, ada_b.reshape(depth, 1, -1), l)[:b]
        ada3 = ada.reshape(b, N_ADA, d)
        h = _ffn(h, ada3, norm1_g[l].reshape(1, d), ffn1_w1[l].astype(BF16), ffn1_w3[l].astype(BF16),
                 ffn1_w2[l].astype(BF16), slot=0, seq=t)

        w_in16 = w_in.astype(BF16)
        w_tail = w_in16[l][:, cut1:]
        w_ba = jnp.zeros((d, LANES), BF16).at[:, :2 * HEADS].set(w_in16[l][:, cut0:cut1])
        proj, ba = _inproj(h, ada3, norm2_g[l].reshape(1, d), w_in16, l, cut0, w_tail, w_ba, seq=t)
        proj3 = proj.reshape(b, t, -1)
        ba3 = ba.reshape(b, t, LANES)
        bat3 = jnp.transpose(ba3, (0, 2, 1))
        y_a = _deltanet(proj3, ba3, bat3, dn_conv_w[l], dn_a_log[l], dn_dt_bias[l], dn_norm_g[l])
        qn, kn, vt, kmean = _moba_prep(proj3, mb_q_norm_g[l], mb_k_norm_g[l])
        y_b = _moba(qn, kn, vt, kmean, rel_bias)
        merged = _merge(y_a.reshape(m, wide), y_b.reshape(m, wide), w_proj_a[l].astype(BF16),
                        w_proj_b[l].astype(BF16), proj, 7)
        gate2 = ada3[:, 5:6, :]
        h = _outproj(merged, w_out[l].astype(BF16), h, gate2, seq=t)

        h = _ffn(h, ada3, norm3_g[l].reshape(1, d), ffn2_w1[l].astype(BF16), ffn2_w3[l].astype(BF16),
                 ffn2_w2[l].astype(BF16), slot=6, seq=t)
    return h.reshape(b, t, d)
```

```python
import functools
import math

import jax
import jax.numpy as jnp
from jax import lax
from jax.experimental import pallas as pl
from jax.experimental.pallas import tpu as pltpu

F32 = jnp.float32
BF16 = jnp.bfloat16

LANES = 128
SUBLANES = 8
NORM_EPS = 1e-6
N_ADA = 9
HEADS = 16
HEAD_DIM = 128
DN_CONV = 4
DN_CHUNK = 128
DN_HEAD_BLOCK = 16
DN_INV_BASE = 8
MB_BLOCK = 256
MB_TOPK = 3
MB_HEAD_BLOCK = 8
MB_QBLOCKS = 1
MB_ISSUE_GROUP = 1
MB_ISSUE_AHEAD = 2
MB_VROWS = HEAD_DIM + 16
RP_BUCKETS = 32
RP_MAX_DIST = 1024
NEG_BIG = -1e30
VMEM_LIMIT = 56 * 1024 * 1024

NT_DIMS = (((1,), (1,)), ((), ()))
NN_DIMS = (((1,), (0,)), ((), ()))
TN_DIMS = (((0,), (0,)), ((), ()))


def _dot(a, b, dims=NN_DIMS):
    return lax.dot_general(a.astype(BF16), b.astype(BF16), dims, preferred_element_type=F32)


def _split_bf16(x):
    hi = x.astype(BF16)
    lo = (x - hi.astype(F32)).astype(BF16)
    return hi, lo


def _dot3(a, b, dims=NN_DIMS):
    ah, al = _split_bf16(a)
    bh, bl = _split_bf16(b)
    f = lambda x, y: lax.dot_general(x, y, dims, preferred_element_type=F32)
    return f(ah, bh) + (f(ah, bl) + f(al, bh))


def _silu(x):
    return x * jax.nn.sigmoid(x)


def _params(*sem):
    return pltpu.CompilerParams(dimension_semantics=sem, vmem_limit_bytes=VMEM_LIMIT)


def _ada_kernel(c_ref, w_ref, b_ref, o_ref):
    s = _silu(c_ref[...])
    o_ref[...] = _dot3(s, w_ref[0]) + b_ref[0]


def _ada(c_pad, w, b, layer):
    rows, d = c_pad.shape
    n = w.shape[2]
    tn = 1024
    return pl.pallas_call(
        _ada_kernel,
        out_shape=jax.ShapeDtypeStruct((rows, n), F32),
        grid=(n // tn,),
        in_specs=[pl.BlockSpec((rows, d), lambda j: (0, 0)),
                  pl.BlockSpec((1, d, tn), lambda j: (layer, 0, j)),
                  pl.BlockSpec((1, 1, tn), lambda j: (layer, 0, j))],
        out_specs=pl.BlockSpec((rows, tn), lambda j: (0, j)),
        compiler_params=_params("arbitrary"),
        name="ada_proj",
    )(c_pad, w, b)


def _norm_modulate(h, gain, shift, scale):
    ms = jnp.mean(h * h, axis=-1, keepdims=True)
    return (h * lax.rsqrt(ms + NORM_EPS)) * (gain * (1.0 + scale)) + shift


def _ffn_kernel(*refs, slot, nf, mixer):
    if mixer:
        h_ref, x_ref, wm_ref, ada_ref, g_ref, w1_ref, w3_ref, w2_ref, o_ref, u_ref, acc_ref = refs
    else:
        h_ref, ada_ref, g_ref, w1_ref, w3_ref, w2_ref, o_ref, u_ref, acc_ref = refs
    f = pl.program_id(1)

    @pl.when(f == 0)
    def _():
        r = h_ref[...]
        if mixer:
            r = r + ada_ref[0, slot - 1:slot, :] * jnp.dot(x_ref[...], wm_ref[...], preferred_element_type=F32)
            o_ref[...] = r
        u = _norm_modulate(r, g_ref[...], ada_ref[0, slot:slot + 1, :], ada_ref[0, slot + 1:slot + 2, :])
        u_ref[...] = u.astype(BF16)
        acc_ref[...] = jnp.zeros_like(acc_ref)

    u = u_ref[...]
    a = jnp.dot(u, w1_ref[...], preferred_element_type=F32)
    b = jnp.dot(u, w3_ref[...], preferred_element_type=F32)
    acc_ref[...] += jnp.dot((_silu(a) * b).astype(BF16), w2_ref[...], preferred_element_type=F32)

    @pl.when(f == nf - 1)
    def _():
        r = o_ref[...] if mixer else h_ref[...]
        o_ref[...] = r + 0.5 * ada_ref[0, slot + 2:slot + 3, :] * acc_ref[...]


def _ffn(h, ada3, gain, w1, w3, w2, *, slot, seq, mix_x=None, mix_w=None):
    m, d = h.shape
    dff = w1.shape[1]
    tm, tf = 512, 512
    nf = dff // tf
    per_batch = seq // tm
    mixer = mix_x is not None
    row = lambda shape: pl.BlockSpec(shape, lambda i, f: (i, 0))
    mix_specs = [row((tm, mix_x.shape[1])),
                 pl.BlockSpec(mix_w.shape, lambda i, f: (0, 0), pipeline_mode=pl.Buffered(1))] if mixer else []
    return pl.pallas_call(
        functools.partial(_ffn_kernel, slot=slot, nf=nf, mixer=mixer),
        out_shape=jax.ShapeDtypeStruct((m, d), F32),
        grid=(m // tm, nf),
        in_specs=[row((tm, d))] + mix_specs + [
                  pl.BlockSpec((1, N_ADA, d), lambda i, f: (i // per_batch, 0, 0)),
                  pl.BlockSpec((1, d), lambda i, f: (0, 0)),
                  pl.BlockSpec((d, tf), lambda i, f: (0, f)),
                  pl.BlockSpec((d, tf), lambda i, f: (0, f)),
                  pl.BlockSpec((tf, d), lambda i, f: (f, 0))],
        out_specs=row((tm, d)),
        scratch_shapes=[pltpu.VMEM((tm, d), BF16), pltpu.VMEM((tm, d), F32)],
        compiler_params=_params("parallel", "arbitrary"),
        name="ffn",
    )(*([h] + ([mix_x, mix_w] if mixer else []) + [ada3, gain, w1, w3, w2]))


def _inproj_kernel(h_ref, ada_ref, g_ref, wa_ref, wb_ref, wba_ref, o_ref, ba_ref, u_ref, *, na):
    j = pl.program_id(1)

    @pl.when(j == 0)
    def _():
        u = _norm_modulate(h_ref[...], g_ref[...], ada_ref[0, 3:4, :], ada_ref[0, 4:5, :])
        u_ref[...] = u.astype(BF16)
        ba_ref[...] = jnp.dot(u_ref[...], wba_ref[...], preferred_element_type=F32)

    @pl.when(j < na)
    def _():
        o_ref[...] = jnp.dot(u_ref[...], wa_ref[0].astype(BF16), preferred_element_type=F32).astype(o_ref.dtype)

    @pl.when(j >= na)
    def _():
        o_ref[...] = jnp.dot(u_ref[...], wb_ref[...], preferred_element_type=F32).astype(o_ref.dtype)


def _inproj(h, ada3, gain, w_all, layer, n_lead, wb, wba, *, seq):
    m, d = h.shape
    tm, tn = 1024, 1024
    na = n_lead // tn
    n = n_lead + wb.shape[1]
    per_batch = seq // tm
    return pl.pallas_call(
        functools.partial(_inproj_kernel, na=na),
        out_shape=(jax.ShapeDtypeStruct((m, n), BF16), jax.ShapeDtypeStruct((m, LANES), F32)),
        grid=(m // tm, n // tn),
        in_specs=[pl.BlockSpec((tm, d), lambda i, j: (i, 0)),
                  pl.BlockSpec((1, N_ADA, d), lambda i, j: (i // per_batch, 0, 0)),
                  pl.BlockSpec((1, d), lambda i, j: (0, 0)),
                  pl.BlockSpec((1, d, tn), lambda i, j: (layer, 0, jnp.minimum(j, na - 1))),
                  pl.BlockSpec((d, tn), lambda i, j: (0, jnp.maximum(j - na, 0))),
                  pl.BlockSpec((d, LANES), lambda i, j: (0, 0))],
        out_specs=(pl.BlockSpec((tm, tn), lambda i, j: (i, j)),
                   pl.BlockSpec((tm, LANES), lambda i, j: (i, 0))),
        scratch_shapes=[pltpu.VMEM((tm, d), BF16)],
        compiler_params=_params("parallel", "arbitrary"),
        name="in_proj",
    )(h, ada3, gain, w_all, wb, wba)


def _softplus(x):
    return jnp.maximum(x, 0.0) + jnp.log1p(jnp.exp(-jnp.abs(x)))


def _dn_kernel(q_ref, k_ref, v_ref, z_ref, ba_ref, bat_ref, wq_ref, wk_ref, wv_ref,
               arow_ref, dtrow_ref, acol_ref, dtcol_ref, gain_ref, o_ref,
               prev_ref, state_ref, gcrow_ref, *, hb):
    c = DN_CHUNK
    hg = pl.program_id(1)
    n = pl.program_id(2)

    @pl.when(n == 0)
    def _():
        prev_ref[...] = jnp.zeros_like(prev_ref)
        state_ref[...] = jnp.zeros_like(state_ref)

    width = hb * HEAD_DIM
    sub = prev_ref.shape[1]
    row_t = lax.broadcasted_iota(jnp.int32, (sub, width), 0)

    def conv_silu(x_ref, w_ref, s):
        cur = x_ref[0].astype(F32)
        tail = prev_ref[s]
        acc = cur * w_ref[DN_CONV - 1:DN_CONV, :]
        for d in range(1, DN_CONV):
            shifted = pltpu.roll(cur, d, 0)
            head = jnp.where(row_t < d, pltpu.roll(tail, d, 0), shifted[:sub])
            shifted = jnp.concatenate([head, shifted[sub:]], axis=0)
            acc = acc + shifted * w_ref[DN_CONV - 1 - d:DN_CONV - d, :]
        prev_ref[s] = cur[c - sub:]
        return _silu(acc)

    xq = conv_silu(q_ref, wq_ref, 0)
    xk = conv_silu(k_ref, wk_ref, 1)
    xv = conv_silu(v_ref, wv_ref, 2)

    slab = ba_ref[0]
    beta_mat = jax.nn.sigmoid(slab)
    g_mat = -jnp.exp(arow_ref[...]) * _softplus(slab + dtrow_ref[...])
    ri = lax.broadcasted_iota(jnp.int32, (c, c), 0)
    ci = lax.broadcasted_iota(jnp.int32, (c, c), 1)
    lower_incl = (ri >= ci)
    tri_l = jnp.where(lower_incl, 1.0, 0.0).astype(BF16)
    tri_u = jnp.where(ri <= ci, 1.0, 0.0).astype(BF16)

    def exact_ones_dot(x, ones, left):
        x1 = x.astype(BF16)
        r1 = x - x1.astype(F32)
        x2 = r1.astype(BF16)
        x3 = (r1 - x2.astype(F32)).astype(BF16)
        if left:
            f = lambda p: jnp.dot(ones, p, preferred_element_type=F32)
        else:
            f = lambda p: jnp.dot(p, ones, preferred_element_type=F32)
        return f(x1) + (f(x2) + f(x3))

    gc_colmat = exact_ones_dot(g_mat, tri_l, True)
    a_rows = bat_ref[0, HEADS:2 * HEADS, :]
    g_rows = -jnp.exp(acol_ref[...]) * _softplus(a_rows + dtcol_ref[...])
    gcrow_ref[...] = exact_ones_dot(g_rows, tri_u, False)

    lane = lax.broadcasted_iota(jnp.int32, (c, LANES), 1)
    eye = jnp.where(ri == ci, 1.0, 0.0)
    strict = ri > ci
    scale = HEAD_DIM ** -0.5

    hs = []
    for hh in range(hb):
        h = hg * hb + hh
        sl = slice(hh * HEAD_DIM, (hh + 1) * HEAD_DIM)
        beta = jnp.sum(jnp.where(lane == h, beta_mat, 0.0), axis=1, keepdims=True)
        gcc = jnp.sum(jnp.where(lane == HEADS + h, gc_colmat, 0.0), axis=1, keepdims=True)
        gcr = gcrow_ref[pl.ds(h, 1), :]
        gc_last = gcr[:, c - 1:c]
        decay = jnp.where(lower_incl, jnp.exp(jnp.minimum(gcc - gcr, 0.0)), 0.0)
        q = xq[:, sl]
        k = xk[:, sl]
        q = q * (lax.rsqrt(jnp.sum(q * q, axis=-1, keepdims=True) + NORM_EPS) * scale)
        k = k * lax.rsqrt(jnp.sum(k * k, axis=-1, keepdims=True) + NORM_EPS)
        kb = k * beta
        egc = jnp.exp(gcc)
        hs.append(dict(hh=hh, sl=sl, decay=decay, q=q, k=k, kb=kb,
                       rhs=jnp.concatenate([xv[:, sl] * beta, kb * egc], axis=1),
                       qd=q * egc, kd=k * jnp.exp(gc_last - gcc), g_tot=jnp.exp(gc_last)))

    for s in hs:
        s["lmat"] = jnp.where(strict, _dot(s["kb"], s["k"], NT_DIMS) * s["decay"], 0.0)
        s["qk"] = _dot(s["q"], s["k"], NT_DIMS) * s["decay"]
    shift0 = int(math.log2(DN_INV_BASE))
    mm = lambda a, b: jnp.dot(a, b, preferred_element_type=F32)
    same_block = jnp.where((ri >> shift0) == (ci >> shift0), 1.0, 0.0).astype(BF16)
    for s in hs:
        s["lb"] = s["lmat"].astype(BF16)
        s["pw"] = -(s["lb"] * same_block)
        s["tmat"] = eye + s["pw"].astype(F32)
    for _ in range(shift0 - 1):
        for s in hs:
            s["pw"] = mm(s["pw"], s["pw"]).astype(BF16)
        for s in hs:
            s["tmat"] = s["tmat"] + mm(s["tmat"].astype(BF16), s["pw"])
    for sh in range(shift0, int(math.log2(c))):
        lower_left = jnp.where(jnp.logical_and((ri >> sh) == (ci >> sh) + 1, ((ci >> sh) & 1) == 0),
                               1.0, 0.0).astype(BF16)
        for s in hs:
            s["tb"] = s["tmat"].astype(BF16)
            s["pw"] = mm(s["lb"] * lower_left, s["tb"]).astype(BF16)
        for s in hs:
            s["tmat"] = s["tmat"] - mm(s["tb"], s["pw"])
    for s in hs:
        s["uw"] = _dot(s["tmat"], s["rhs"])
    for s in hs:
        s["state"] = state_ref[s["hh"]]
        s["v_new"] = s["uw"][:, :HEAD_DIM] - _dot(s["uw"][:, HEAD_DIM:], s["state"])
    for s in hs:
        s["o"] = _dot(s["qd"], s["state"]) + _dot(s["qk"], s["v_new"])
        state_ref[s["hh"]] = s["state"] * s["g_tot"] + _dot(s["kd"], s["v_new"], TN_DIMS)
    for s in hs:
        o = s["o"]
        o = o * lax.rsqrt(jnp.mean(o * o, axis=-1, keepdims=True) + NORM_EPS) * gain_ref[...]
        o_ref[0, :, s["sl"]] = (o * _silu(z_ref[0, :, s["sl"]].astype(F32))).astype(o_ref.dtype)


def _deltanet(proj3, ba3, bat3, conv_w, a_log, dt_bias, gain):
    b, t, _ = proj3.shape
    hb = DN_HEAD_BLOCK
    c = DN_CHUNK
    width = hb * HEAD_DIM
    ng = HEADS // hb
    region = (HEADS * HEAD_DIM) // width

    def pad_row(x):
        return jnp.zeros((1, LANES), F32).at[0, HEADS:2 * HEADS].set(x.astype(F32))

    arow, dtrow = pad_row(a_log), pad_row(dt_bias)
    acol = a_log.astype(F32).reshape(HEADS, 1)
    dtcol = dt_bias.astype(F32).reshape(HEADS, 1)
    gain2 = gain.astype(F32).reshape(1, HEAD_DIM)

    def col(r):
        return pl.BlockSpec((1, c, width), lambda bi, g, n, r=r: (bi, n, r * region + g))

    def wcol(r):
        return pl.BlockSpec((DN_CONV, width), lambda bi, g, n, r=r: (0, r * region + g))

    const = lambda shape: pl.BlockSpec(shape, lambda bi, g, n: tuple(0 for _ in shape))
    return pl.pallas_call(
        functools.partial(_dn_kernel, hb=hb),
        out_shape=jax.ShapeDtypeStruct((b, t, HEADS * HEAD_DIM), BF16),
        grid=(b, ng, t // c),
        in_specs=[col(0), col(1), col(2), col(3),
                  pl.BlockSpec((1, c, LANES), lambda bi, g, n: (bi, n, 0)),
                  pl.BlockSpec((1, LANES, c), lambda bi, g, n: (bi, 0, n)),
                  wcol(0), wcol(1), wcol(2),
                  const((1, LANES)), const((1, LANES)), const((HEADS, 1)), const((HEADS, 1)),
                  const((1, HEAD_DIM))],
        out_specs=pl.BlockSpec((1, c, width), lambda bi, g, n: (bi, n, g)),
        scratch_shapes=[pltpu.VMEM((3, SUBLANES, width), F32),
                        pltpu.VMEM((hb, HEAD_DIM, HEAD_DIM), F32),
                        pltpu.VMEM((HEADS, c), F32)],
        compiler_params=_params("parallel", "parallel", "arbitrary"),
        name="deltanet",
    )(proj3, proj3, proj3, proj3, ba3, bat3, conv_w, conv_w, conv_w, arow, dtrow, acol, dtcol, gain2)


def _moba_prep_kernel(q_ref, k_ref, v_ref, qg_ref, kg_ref, qn_ref, kn_ref, vt_ref, km_ref):
    n = pl.program_id(1)
    scale = HEAD_DIM ** -0.5

    @pl.when(n == 0)
    def _():
        km_ref[...] = jnp.zeros_like(km_ref)

    row = lax.broadcasted_iota(jnp.int32, (km_ref.shape[1], HEAD_DIM), 0)
    ones = jnp.ones((MB_VROWS - HEAD_DIM, MB_BLOCK), BF16)
    lane = lax.broadcasted_iota(jnp.int32, (MB_BLOCK, LANES), 1)
    onehot = jnp.where(lane == n, 1.0, 0.0).astype(BF16)
    for h in range(HEADS):
        sl = slice(h * HEAD_DIM, (h + 1) * HEAD_DIM)
        ka = 2 * h * HEAD_DIM
        q = q_ref[0, :, sl].astype(F32)
        k = k_ref[0, :, sl].astype(F32)
        qn = q * lax.rsqrt(jnp.mean(q * q, axis=-1, keepdims=True) + NORM_EPS) * qg_ref[...]
        kn = k * lax.rsqrt(jnp.mean(k * k, axis=-1, keepdims=True) + NORM_EPS) * kg_ref[...]
        qn_ref[0, :, sl] = (qn * scale).astype(qn_ref.dtype)
        kn_ref[0, 0, :, ka:ka + HEAD_DIM] = kn.astype(BF16)
        kn_ref[0, 0, :, ka + HEAD_DIM:ka + 2 * HEAD_DIM] = onehot
        km_ref[0, :, sl] = jnp.where(row == n, jnp.mean(kn, axis=0, keepdims=True), km_ref[0, :, sl])
        vt_ref[0, 0, h, :HEAD_DIM, :] = v_ref[0, :, sl].astype(F32).T.astype(BF16)
        vt_ref[0, 0, h, HEAD_DIM:, :] = ones


def _moba_prep(proj3, q_gain, k_gain):
    b, t, _ = proj3.shape
    nb = t // MB_BLOCK
    w = HEADS * HEAD_DIM
    col = lambda r: pl.BlockSpec((1, MB_BLOCK, w), lambda bi, n, r=r: (bi, n, r))
    gspec = pl.BlockSpec((1, HEAD_DIM), lambda bi, n: (0, 0))
    return pl.pallas_call(
        _moba_prep_kernel,
        out_shape=(jax.ShapeDtypeStruct((b, t, w), BF16),
                   jax.ShapeDtypeStruct((b, nb, MB_BLOCK, 2 * w), BF16),
                   jax.ShapeDtypeStruct((b, nb, HEADS, MB_VROWS, MB_BLOCK), BF16),
                   jax.ShapeDtypeStruct((b, nb, w), F32)),
        grid=(b, nb),
        in_specs=[col(4), col(5), col(6), gspec, gspec],
        out_specs=(pl.BlockSpec((1, MB_BLOCK, w), lambda bi, n: (bi, n, 0)),
                   pl.BlockSpec((1, 1, MB_BLOCK, 2 * w), lambda bi, n: (bi, n, 0, 0)),
                   pl.BlockSpec((1, 1, HEADS, MB_VROWS, MB_BLOCK), lambda bi, n: (bi, n, 0, 0, 0)),
                   pl.BlockSpec((1, nb, w), lambda bi, n: (bi, 0, 0))),
        compiler_params=_params("parallel", "arbitrary"),
        name="moba_prep",
    )(proj3, proj3, proj3, q_gain.astype(F32).reshape(1, HEAD_DIM), k_gain.astype(F32).reshape(1, HEAD_DIM))


def _bucket_thresholds():
    max_exact = RP_BUCKETS // 2
    th = list(range(1, max_exact + 1))
    for kk in range(1, RP_BUCKETS - max_exact):
        x = max_exact * (RP_MAX_DIST / max_exact) ** (kk / (RP_BUCKETS - max_exact))
        r = round(x)
        th.append(r if abs(x - r) < 1e-9 else math.ceil(x))
    return th


N_NEAR = 5
LOG2E = 1.4426950408889634


def _moba_kernel(rb_ref, q_ref, k_ref, vt_ref, km_ref, o_ref, tab_ref, qa_ref, m_ref, alpha_ref, acc_ref, p_ref,
                 *, nb, hb):
    hg = pl.program_id(0)
    bi = pl.program_id(1)
    it = pl.program_id(2)
    blk = MB_BLOCK
    nq = MB_QBLOCKS
    tq = nq * blk
    shift = int(math.log2(blk))
    last = nq * it + nq - 1
    thresholds = _bucket_thresholds()
    heads = list(range(hb))
    hsl = lambda hh: slice(hh * HEAD_DIM, (hh + 1) * HEAD_DIM)

    @pl.when(jnp.logical_and(bi == 0, it == 0))
    def _():
        ko = lax.broadcasted_iota(jnp.int32, (blk, blk), 0)
        qo = lax.broadcasted_iota(jnp.int32, (blk, blk), 1)
        for hh in heads:
            h = hg * hb + hh
            far = rb_ref[RP_BUCKETS - 1, h]
            for delta in range(N_NEAR):
                d = delta * blk + qo - ko
                acc = jnp.full((blk, blk), rb_ref[0, h], F32)
                for bkt in range(1, RP_BUCKETS):
                    if delta * blk + blk - 1 >= thresholds[bkt - 1]:
                        acc = jnp.where(d >= thresholds[bkt - 1], rb_ref[bkt, h], acc)
                acc = (acc - far) * LOG2E
                if delta == 0:
                    acc = jnp.where(d >= 0, acc, NEG_BIG)
                tab_ref[hh, delta] = acc
            tab_ref[hh, N_NEAR] = jnp.zeros((blk, blk), F32)

    bidx = lax.broadcasted_iota(jnp.int32, (nb, tq), 0)
    qblk = nq * it + (lax.broadcasted_iota(jnp.int32, (nb, tq), 1) >> shift)
    past = bidx < qblk
    gates = []
    for hh in heads:
        kmh, kml = _split_bf16(km_ref[0, :, hsl(hh)])
        qb = q_ref[0, :, hsl(hh)]
        gate = (lax.dot_general(kmh, qb, NT_DIMS, preferred_element_type=F32)
                + lax.dot_general(kml, qb, NT_DIMS, preferred_element_type=F32))
        gates.append(jnp.where(past, gate, -jnp.inf))
    for hh, gate in zip(heads, gates):
        rank = jnp.zeros((nb, tq), jnp.int32)
        for jp in range(nb):
            gj = gate[jp:jp + 1, :]
            ahead = jnp.logical_or(gj > gate, jnp.logical_and(gj == gate, jp < bidx))
            rank = rank + ahead.astype(jnp.int32)
        allowed = jnp.logical_or(jnp.logical_and(past, rank < MB_TOPK), bidx == qblk)
        mask = jnp.where(allowed, 0.0, NEG_BIG)
        q_t = (q_ref[0, :, hsl(hh)].astype(F32) * LOG2E).T.astype(BF16)
        mask_rows = jnp.concatenate([mask, jnp.zeros((LANES - nb, tq), F32)], axis=0).astype(BF16)
        qa_ref[hh] = jnp.concatenate([q_t, mask_rows], axis=0)

    m_ref[...] = jnp.full(m_ref.shape, NEG_BIG, F32)
    acc_ref[...] = jnp.zeros(acc_ref.shape, F32)

    def logits(t, near, hs):
        j = last - t
        ss = [jnp.dot(k_ref[0, j, :, 2 * hh * HEAD_DIM:2 * (hh + 1) * HEAD_DIM], qa_ref[hh],
                      preferred_element_type=F32) for hh in hs]
        if near:
            idx = [jnp.clip(t - (nq - 1 - a), 0, N_NEAR) for a in range(nq)]
            ss = [jnp.concatenate([s[:, a * blk:(a + 1) * blk] + tab_ref[hh, idx[a]] for a in range(nq)], axis=1)
                  for hh, s in zip(hs, ss)]
        return ss

    def values(t, hs):
        slot = t & 1
        return [jnp.dot(vt_ref[0, last - t, hh], p_ref[slot, hh], preferred_element_type=F32) for hh in hs]

    def probabilities(t, ss, hs):
        slot = t & 1
        alphas = []
        for hh, s in zip(hs, ss):
            for q0 in range(0, tq, LANES):
                qs = slice(q0, q0 + LANES)
                m_old = m_ref[hh, :, qs]
                m_new = jnp.maximum(m_old, jnp.max(s[:, qs], axis=0, keepdims=True))
                m_ref[hh, :, qs] = m_new
                p_ref[slot, hh, :, qs] = jnp.exp2(s[:, qs] - m_new).astype(BF16)
                alphas.append((hh, qs, jnp.exp2(m_old - m_new)))
        return alphas

    def accumulate(pvs, hs):
        for hh, pv in zip(hs, pvs):
            acc_ref[hh] = alpha_ref[hh] * acc_ref[hh] + pv

    def set_alpha(alphas):
        for hh, qs, alpha in alphas:
            alpha_ref[hh, :, qs] = alpha

    groups = [heads[g:g + MB_ISSUE_GROUP] for g in range(0, hb, MB_ISSUE_GROUP)]

    def make_body(near):
        def body(t, carry):
            ahead = [logits(t, near, grp) for grp in groups[:MB_ISSUE_AHEAD]]
            for gi, grp in enumerate(groups):
                if gi + MB_ISSUE_AHEAD < len(groups):
                    ahead.append(logits(t, near, groups[gi + MB_ISSUE_AHEAD]))
                pvs = values(t - 1, grp)
                alphas = probabilities(t, ahead[gi], grp)
                accumulate(pvs, grp)
                set_alpha(alphas)
            return carry
        return body

    set_alpha(probabilities(0, logits(0, True, heads), heads))
    n_near = jnp.minimum(last + 1, N_NEAR + nq - 1)
    lax.fori_loop(1, n_near, make_body(True), 0)
    lax.fori_loop(n_near, last + 1, make_body(False), 0)
    accumulate(values(last, heads), heads)
    for hh in heads:
        acc = acc_ref[hh]
        o_ref[0, :, hsl(hh)] = (acc[:HEAD_DIM] / acc[HEAD_DIM:HEAD_DIM + 1]).T.astype(o_ref.dtype)


def _moba(qn, kn, vt, kmean, rel_bias):
    b, t, w = qn.shape
    nb = t // MB_BLOCK
    hb = MB_HEAD_BLOCK
    hw = hb * HEAD_DIM
    tq = MB_QBLOCKS * MB_BLOCK
    return pl.pallas_call(
        functools.partial(_moba_kernel, nb=nb, hb=hb),
        out_shape=jax.ShapeDtypeStruct((b, t, w), BF16),
        grid=(HEADS // hb, b, t // tq),
        in_specs=[pl.BlockSpec(memory_space=pltpu.SMEM),
                  pl.BlockSpec((1, tq, hw), lambda g, bi, i: (bi, i, g)),
                  pl.BlockSpec((1, nb, MB_BLOCK, 2 * hw), lambda g, bi, i: (bi, 0, 0, g),
                               pipeline_mode=pl.Buffered(1)),
                  pl.BlockSpec((1, nb, hb, MB_VROWS, MB_BLOCK), lambda g, bi, i: (bi, 0, g, 0, 0),
                               pipeline_mode=pl.Buffered(1)),
                  pl.BlockSpec((1, nb, hw), lambda g, bi, i: (bi, 0, g))],
        out_specs=pl.BlockSpec((1, tq, hw), lambda g, bi, i: (bi, i, g)),
        scratch_shapes=[pltpu.VMEM((hb, N_NEAR + 1, MB_BLOCK, MB_BLOCK), F32),
                        pltpu.VMEM((hb, 2 * HEAD_DIM, tq), BF16),
                        pltpu.VMEM((hb, 1, tq), F32),
                        pltpu.VMEM((hb, 1, tq), F32),
                        pltpu.VMEM((hb, MB_VROWS, tq), F32),
                        pltpu.VMEM((2, hb, MB_BLOCK, tq), BF16)],
        compiler_params=_params("arbitrary", "arbitrary", "arbitrary"),
        name="moba_attn",
    )(rel_bias.astype(F32), qn, kn, vt, kmean)


def _merge_kernel(ya_ref, yb_ref, wa_ref, wb_ref, ga_ref, gb_ref, o_ref):
    pa = jnp.dot(ya_ref[...], wa_ref[...], preferred_element_type=F32)
    pb = jnp.dot(yb_ref[...], wb_ref[...], preferred_element_type=F32)
    ga = ga_ref[...].astype(F32)
    gb = gb_ref[...].astype(F32)
    o_ref[...] = (jax.nn.sigmoid(ga) * pa + jax.nn.sigmoid(gb) * pb).astype(o_ref.dtype)


def _merge(ya, yb, wa, wb, proj, gate_col):
    m, d = ya.shape
    tm, tn = 1024, 512
    gpb = d // tn
    return pl.pallas_call(
        _merge_kernel,
        out_shape=jax.ShapeDtypeStruct((m, d), BF16),
        grid=(m // tm, d // tn),
        in_specs=[pl.BlockSpec((tm, d), lambda i, j: (i, 0)),
                  pl.BlockSpec((tm, d), lambda i, j: (i, 0)),
                  pl.BlockSpec((d, tn), lambda i, j: (0, j)),
                  pl.BlockSpec((d, tn), lambda i, j: (0, j)),
                  pl.BlockSpec((tm, tn), lambda i, j: (i, gate_col * gpb + j)),
                  pl.BlockSpec((tm, tn), lambda i, j: (i, (gate_col + 1) * gpb + j))],
        out_specs=pl.BlockSpec((tm, tn), lambda i, j: (i, j)),
        compiler_params=_params("parallel", "arbitrary"),
        name="merge",
    )(ya, yb, wa, wb, proj, proj)


def kernel(x, c, ada_w, ada_b, norm1_g, ffn1_w1, ffn1_w3, ffn1_w2, norm2_g, w_in, dn_conv_w, dn_a_log, dn_dt_bias, dn_norm_g, mb_q_norm_g, mb_k_norm_g, rel_bias, w_proj_a, w_proj_b, w_out, norm3_g, ffn2_w1, ffn2_w3, ffn2_w2):
    b, t, d = x.shape
    m = b * t
    depth = ada_w.shape[0]
    wide = HEADS * HEAD_DIM
    cut0 = 4 * wide
    cut1 = cut0 + 2 * HEADS
    h = x.reshape(m, d)
    c_pad = jnp.zeros((8, d), F32).at[:b].set(c)
    for l in range(depth):
        ada = _ada(c_pad, ada_w, ada_b.reshape(depth, 1, -1), l)[:b]
        ada3 = ada.reshape(b, N_ADA, d)
        h = _ffn(h, ada3, norm1_g[l].reshape(1, d), ffn1_w1[l].astype(BF16), ffn1_w3[l].astype(BF16),
                 ffn1_w2[l].astype(BF16), slot=0, seq=t)

        w_tail = w_in[l][:, cut1:].astype(BF16)
        w_ba = jnp.zeros((d, LANES), BF16).at[:, :2 * HEADS].set(w_in[l][:, cut0:cut1].astype(BF16))
        proj, ba = _inproj(h, ada3, norm2_g[l].reshape(1, d), w_in, l, cut0, w_tail, w_ba, seq=t)
        proj3 = proj.reshape(b, t, -1)
        ba3 = ba.reshape(b, t, LANES)
        bat3 = jnp.transpose(ba3, (0, 2, 1))
        y_a = _deltanet(proj3, ba3, bat3, dn_conv_w[l], dn_a_log[l], dn_dt_bias[l], dn_norm_g[l])
        qn, kn, vt, kmean = _moba_prep(proj3, mb_q_norm_g[l], mb_k_norm_g[l])
        y_b = _moba(qn, kn, vt, kmean, rel_bias)
        merged = _merge(y_a.reshape(m, wide), y_b.reshape(m, wide), w_proj_a[l].astype(BF16),
                        w_proj_b[l].astype(BF16), proj, 7)
        h = _ffn(h, ada3, norm3_g[l].reshape(1, d), ffn2_w1[l].astype(BF16), ffn2_w3[l].astype(BF16),
                 ffn2_w2[l].astype(BF16), slot=6, seq=t, mix_x=merged, mix_w=w_out[l].astype(BF16))
    return h.reshape(b, t, d)
```

```python
import functools
import math

import jax
import jax.numpy as jnp
from jax import lax
from jax.experimental import pallas as pl
from jax.experimental.pallas import tpu as pltpu

F32 = jnp.float32
BF16 = jnp.bfloat16

LANES = 128
SUBLANES = 8
NORM_EPS = 1e-6
N_ADA = 9
HEADS = 16
HEAD_DIM = 128
DN_CONV = 4
DN_CHUNK = 128
DN_HEAD_BLOCK = 16
DN_INV_BASE = 8
MB_BLOCK = 256
MB_TOPK = 3
MB_HEAD_BLOCK = 8
MB_QBLOCKS = 1
MB_ISSUE_GROUP = 1
MB_ISSUE_AHEAD = 2
MB_VROWS = HEAD_DIM + 16
RP_BUCKETS = 32
RP_MAX_DIST = 1024
NEG_BIG = -1e30
VMEM_LIMIT = 56 * 1024 * 1024

NT_DIMS = (((1,), (1,)), ((), ()))
NN_DIMS = (((1,), (0,)), ((), ()))
TN_DIMS = (((0,), (0,)), ((), ()))


def _dot(a, b, dims=NN_DIMS):
    return lax.dot_general(a.astype(BF16), b.astype(BF16), dims, preferred_element_type=F32)


def _split_bf16(x):
    hi = x.astype(BF16)
    lo = (x - hi.astype(F32)).astype(BF16)
    return hi, lo


def _dot3(a, b, dims=NN_DIMS):
    ah, al = _split_bf16(a)
    bh, bl = _split_bf16(b)
    f = lambda x, y: lax.dot_general(x, y, dims, preferred_element_type=F32)
    return f(ah, bh) + (f(ah, bl) + f(al, bh))


def _silu(x):
    return x * jax.nn.sigmoid(x)


def _params(*sem):
    return pltpu.CompilerParams(dimension_semantics=sem, vmem_limit_bytes=VMEM_LIMIT)


def _ada_kernel(c_ref, w_ref, b_ref, o_ref):
    s = _silu(c_ref[...])
    o_ref[...] = _dot3(s, w_ref[0]) + b_ref[0]


def _ada(c_pad, w, b, layer):
    rows, d = c_pad.shape
    n = w.shape[2]
    tn = 1024
    return pl.pallas_call(
        _ada_kernel,
        out_shape=jax.ShapeDtypeStruct((rows, n), F32),
        grid=(n // tn,),
        in_specs=[pl.BlockSpec((rows, d), lambda j: (0, 0)),
                  pl.BlockSpec((1, d, tn), lambda j: (layer, 0, j)),
                  pl.BlockSpec((1, 1, tn), lambda j: (layer, 0, j))],
        out_specs=pl.BlockSpec((rows, tn), lambda j: (0, j)),
        compiler_params=_params("arbitrary"),
        name="ada_proj",
    )(c_pad, w, b)


def _norm_modulate(h, gain, shift, scale):
    ms = jnp.mean(h * h, axis=-1, keepdims=True)
    return (h * lax.rsqrt(ms + NORM_EPS)) * (gain * (1.0 + scale)) + shift


def _ffn_kernel(*refs, slot, nf, mixer):
    if mixer:
        h_ref, x_ref, wm_ref, ada_ref, g_ref, w1_ref, w3_ref, w2_ref, o_ref, u_ref, acc_ref = refs
    else:
        h_ref, ada_ref, g_ref, w1_ref, w3_ref, w2_ref, o_ref, u_ref, acc_ref = refs
    f = pl.program_id(1)

    @pl.when(f == 0)
    def _():
        r = h_ref[...]
        if mixer:
            r = r + ada_ref[0, slot - 1:slot, :] * jnp.dot(x_ref[...], wm_ref[...], preferred_element_type=F32)
            o_ref[...] = r
        u = _norm_modulate(r, g_ref[...], ada_ref[0, slot:slot + 1, :], ada_ref[0, slot + 1:slot + 2, :])
        u_ref[...] = u.astype(BF16)
        acc_ref[...] = jnp.zeros_like(acc_ref)

    u = u_ref[...]
    a = jnp.dot(u, w1_ref[...], preferred_element_type=F32)
    b = jnp.dot(u, w3_ref[...], preferred_element_type=F32)
    acc_ref[...] += jnp.dot((_silu(a) * b).astype(BF16), w2_ref[...], preferred_element_type=F32)

    @pl.when(f == nf - 1)
    def _():
        r = o_ref[...] if mixer else h_ref[...]
        o_ref[...] = r + 0.5 * ada_ref[0, slot + 2:slot + 3, :] * acc_ref[...]


def _ffn(h, ada3, gain, w1, w3, w2, *, slot, seq, mix_x=None, mix_w=None):
    m, d = h.shape
    dff = w1.shape[1]
    tm, tf = 512, 512
    nf = dff // tf
    per_batch = seq // tm
    mixer = mix_x is not None
    row = lambda shape: pl.BlockSpec(shape, lambda i, f: (i, 0))
    mix_specs = [row((tm, mix_x.shape[1])),
                 pl.BlockSpec(mix_w.shape, lambda i, f: (0, 0), pipeline_mode=pl.Buffered(1))] if mixer else []
    return pl.pallas_call(
        functools.partial(_ffn_kernel, slot=slot, nf=nf, mixer=mixer),
        out_shape=jax.ShapeDtypeStruct((m, d), F32),
        grid=(m // tm, nf),
        in_specs=[row((tm, d))] + mix_specs + [
                  pl.BlockSpec((1, N_ADA, d), lambda i, f: (i // per_batch, 0, 0)),
                  pl.BlockSpec((1, d), lambda i, f: (0, 0)),
                  pl.BlockSpec((d, tf), lambda i, f: (0, f)),
                  pl.BlockSpec((d, tf), lambda i, f: (0, f)),
                  pl.BlockSpec((tf, d), lambda i, f: (f, 0))],
        out_specs=row((tm, d)),
        scratch_shapes=[pltpu.VMEM((tm, d), BF16), pltpu.VMEM((tm, d), F32)],
        compiler_params=_params("parallel", "arbitrary"),
        name="ffn",
    )(*([h] + ([mix_x, mix_w] if mixer else []) + [ada3, gain, w1, w3, w2]))


def _inproj_kernel(h_ref, ada_ref, g_ref, wa_ref, wb_ref, wba_ref, o_ref, ba_ref, u_ref, *, na):
    j = pl.program_id(1)

    @pl.when(j == 0)
    def _():
        u = _norm_modulate(h_ref[...], g_ref[...], ada_ref[0, 3:4, :], ada_ref[0, 4:5, :])
        u_ref[...] = u.astype(BF16)
        ba_ref[...] = jnp.dot(u_ref[...], wba_ref[...], preferred_element_type=F32)

    @pl.when(j < na)
    def _():
        o_ref[...] = jnp.dot(u_ref[...], wa_ref[0], preferred_element_type=F32).astype(o_ref.dtype)

    @pl.when(j >= na)
    def _():
        o_ref[...] = jnp.dot(u_ref[...], wb_ref[...], preferred_element_type=F32).astype(o_ref.dtype)


def _inproj(h, ada3, gain, w_all, layer, n_lead, wb, wba, *, seq):
    m, d = h.shape
    tm, tn = 1024, 1024
    na = n_lead // tn
    n = n_lead + wb.shape[1]
    per_batch = seq // tm
    return pl.pallas_call(
        functools.partial(_inproj_kernel, na=na),
        out_shape=(jax.ShapeDtypeStruct((m, n), BF16), jax.ShapeDtypeStruct((m, LANES), F32)),
        grid=(m // tm, n // tn),
        in_specs=[pl.BlockSpec((tm, d), lambda i, j: (i, 0)),
                  pl.BlockSpec((1, N_ADA, d), lambda i, j: (i // per_batch, 0, 0)),
                  pl.BlockSpec((1, d), lambda i, j: (0, 0)),
                  pl.BlockSpec((1, d, tn), lambda i, j: (layer, 0, jnp.minimum(j, na - 1))),
                  pl.BlockSpec((d, tn), lambda i, j: (0, jnp.maximum(j - na, 0))),
                  pl.BlockSpec((d, LANES), lambda i, j: (0, 0))],
        out_specs=(pl.BlockSpec((tm, tn), lambda i, j: (i, j)),
                   pl.BlockSpec((tm, LANES), lambda i, j: (i, 0))),
        scratch_shapes=[pltpu.VMEM((tm, d), BF16)],
        compiler_params=_params("parallel", "arbitrary"),
        name="in_proj",
    )(h, ada3, gain, w_all, wb, wba)


def _softplus(x):
    return jnp.maximum(x, 0.0) + jnp.log1p(jnp.exp(-jnp.abs(x)))


def _dn_kernel(q_ref, k_ref, v_ref, z_ref, ba_ref, bat_ref, wq_ref, wk_ref, wv_ref,
               arow_ref, dtrow_ref, acol_ref, dtcol_ref, gain_ref, o_ref,
               prev_ref, state_ref, gcrow_ref, *, hb):
    c = DN_CHUNK
    hg = pl.program_id(1)
    n = pl.program_id(2)

    @pl.when(n == 0)
    def _():
        prev_ref[...] = jnp.zeros_like(prev_ref)
        state_ref[...] = jnp.zeros_like(state_ref)

    width = hb * HEAD_DIM
    sub = prev_ref.shape[1]
    row_t = lax.broadcasted_iota(jnp.int32, (sub, width), 0)

    def conv_silu(x_ref, w_ref, s):
        cur = x_ref[0].astype(F32)
        tail = prev_ref[s]
        acc = cur * w_ref[DN_CONV - 1:DN_CONV, :]
        for d in range(1, DN_CONV):
            shifted = pltpu.roll(cur, d, 0)
            head = jnp.where(row_t < d, pltpu.roll(tail, d, 0), shifted[:sub])
            shifted = jnp.concatenate([head, shifted[sub:]], axis=0)
            acc = acc + shifted * w_ref[DN_CONV - 1 - d:DN_CONV - d, :]
        prev_ref[s] = cur[c - sub:]
        return _silu(acc)

    xq = conv_silu(q_ref, wq_ref, 0)
    xk = conv_silu(k_ref, wk_ref, 1)
    xv = conv_silu(v_ref, wv_ref, 2)

    slab = ba_ref[0]
    beta_mat = jax.nn.sigmoid(slab)
    g_mat = -jnp.exp(arow_ref[...]) * _softplus(slab + dtrow_ref[...])
    ri = lax.broadcasted_iota(jnp.int32, (c, c), 0)
    ci = lax.broadcasted_iota(jnp.int32, (c, c), 1)
    lower_incl = (ri >= ci)
    tri_l = jnp.where(lower_incl, 1.0, 0.0).astype(BF16)
    tri_u = jnp.where(ri <= ci, 1.0, 0.0).astype(BF16)

    def exact_ones_dot(x, ones, left):
        x1 = x.astype(BF16)
        r1 = x - x1.astype(F32)
        x2 = r1.astype(BF16)
        x3 = (r1 - x2.astype(F32)).astype(BF16)
        if left:
            f = lambda p: jnp.dot(ones, p, preferred_element_type=F32)
        else:
            f = lambda p: jnp.dot(p, ones, preferred_element_type=F32)
        return f(x1) + (f(x2) + f(x3))

    gc_colmat = exact_ones_dot(g_mat, tri_l, True)
    a_rows = bat_ref[0, HEADS:2 * HEADS, :]
    g_rows = -jnp.exp(acol_ref[...]) * _softplus(a_rows + dtcol_ref[...])
    gcrow_ref[...] = exact_ones_dot(g_rows, tri_u, False)

    lane = lax.broadcasted_iota(jnp.int32, (c, LANES), 1)
    eye = jnp.where(ri == ci, 1.0, 0.0)
    strict = ri > ci
    scale = HEAD_DIM ** -0.5

    hs = []
    for hh in range(hb):
        h = hg * hb + hh
        sl = slice(hh * HEAD_DIM, (hh + 1) * HEAD_DIM)
        beta = jnp.sum(jnp.where(lane == h, beta_mat, 0.0), axis=1, keepdims=True)
        gcc = jnp.sum(jnp.where(lane == HEADS + h, gc_colmat, 0.0), axis=1, keepdims=True)
        gcr = gcrow_ref[pl.ds(h, 1), :]
        gc_last = gcr[:, c - 1:c]
        decay = jnp.where(lower_incl, jnp.exp(jnp.minimum(gcc - gcr, 0.0)), 0.0)
        q = xq[:, sl]
        k = xk[:, sl]
        q = q * (lax.rsqrt(jnp.sum(q * q, axis=-1, keepdims=True) + NORM_EPS) * scale)
        k = k * lax.rsqrt(jnp.sum(k * k, axis=-1, keepdims=True) + NORM_EPS)
        kb = k * beta
        egc = jnp.exp(gcc)
        hs.append(dict(hh=hh, sl=sl, decay=decay, q=q, k=k, kb=kb,
                       rhs=jnp.concatenate([xv[:, sl] * beta, kb * egc], axis=1),
                       qd=q * egc, kd=k * jnp.exp(gc_last - gcc), g_tot=jnp.exp(gc_last)))

    for s in hs:
        s["lmat"] = jnp.where(strict, _dot(s["kb"], s["k"], NT_DIMS) * s["decay"], 0.0)
        s["qk"] = _dot(s["q"], s["k"], NT_DIMS) * s["decay"]
    shift0 = int(math.log2(DN_INV_BASE))
    mm = lambda a, b: jnp.dot(a, b, preferred_element_type=F32)
    same_block = jnp.where((ri >> shift0) == (ci >> shift0), 1.0, 0.0).astype(BF16)
    for s in hs:
        s["lb"] = s["lmat"].astype(BF16)
        s["pw"] = -(s["lb"] * same_block)
        s["tmat"] = eye + s["pw"].astype(F32)
    for _ in range(shift0 - 1):
        for s in hs:
            s["pw"] = mm(s["pw"], s["pw"]).astype(BF16)
        for s in hs:
            s["tmat"] = s["tmat"] + mm(s["tmat"].astype(BF16), s["pw"])
    for sh in range(shift0, int(math.log2(c))):
        lower_left = jnp.where(jnp.logical_and((ri >> sh) == (ci >> sh) + 1, ((ci >> sh) & 1) == 0),
                               1.0, 0.0).astype(BF16)
        for s in hs:
            s["tb"] = s["tmat"].astype(BF16)
            s["pw"] = mm(s["lb"] * lower_left, s["tb"]).astype(BF16)
        for s in hs:
            s["tmat"] = s["tmat"] - mm(s["tb"], s["pw"])
    for s in hs:
        s["uw"] = _dot(s["tmat"], s["rhs"])
    for s in hs:
        s["state"] = state_ref[s["hh"]]
        s["v_new"] = s["uw"][:, :HEAD_DIM] - _dot(s["uw"][:, HEAD_DIM:], s["state"])
    for s in hs:
        s["o"] = _dot(s["qd"], s["state"]) + _dot(s["qk"], s["v_new"])
        state_ref[s["hh"]] = s["state"] * s["g_tot"] + _dot(s["kd"], s["v_new"], TN_DIMS)
    for s in hs:
        o = s["o"]
        o = o * lax.rsqrt(jnp.mean(o * o, axis=-1, keepdims=True) + NORM_EPS) * gain_ref[...]
        o_ref[0, :, s["sl"]] = (o * _silu(z_ref[0, :, s["sl"]].astype(F32))).astype(o_ref.dtype)


def _deltanet(proj3, ba3, bat3, conv_w, a_log, dt_bias, gain):
    b, t, _ = proj3.shape
    hb = DN_HEAD_BLOCK
    c = DN_CHUNK
    width = hb * HEAD_DIM
    ng = HEADS // hb
    region = (HEADS * HEAD_DIM) // width

    def pad_row(x):
        return jnp.zeros((1, LANES), F32).at[0, HEADS:2 * HEADS].set(x.astype(F32))

    arow, dtrow = pad_row(a_log), pad_row(dt_bias)
    acol = a_log.astype(F32).reshape(HEADS, 1)
    dtcol = dt_bias.astype(F32).reshape(HEADS, 1)
    gain2 = gain.astype(F32).reshape(1, HEAD_DIM)

    def col(r):
        return pl.BlockSpec((1, c, width), lambda bi, g, n, r=r: (bi, n, r * region + g))

    def wcol(r):
        return pl.BlockSpec((DN_CONV, width), lambda bi, g, n, r=r: (0, r * region + g))

    const = lambda shape: pl.BlockSpec(shape, lambda bi, g, n: tuple(0 for _ in shape))
    return pl.pallas_call(
        functools.partial(_dn_kernel, hb=hb),
        out_shape=jax.ShapeDtypeStruct((b, t, HEADS * HEAD_DIM), BF16),
        grid=(b, ng, t // c),
        in_specs=[col(0), col(1), col(2), col(3),
                  pl.BlockSpec((1, c, LANES), lambda bi, g, n: (bi, n, 0)),
                  pl.BlockSpec((1, LANES, c), lambda bi, g, n: (bi, 0, n)),
                  wcol(0), wcol(1), wcol(2),
                  const((1, LANES)), const((1, LANES)), const((HEADS, 1)), const((HEADS, 1)),
                  const((1, HEAD_DIM))],
        out_specs=pl.BlockSpec((1, c, width), lambda bi, g, n: (bi, n, g)),
        scratch_shapes=[pltpu.VMEM((3, SUBLANES, width), F32),
                        pltpu.VMEM((hb, HEAD_DIM, HEAD_DIM), F32),
                        pltpu.VMEM((HEADS, c), F32)],
        compiler_params=_params("parallel", "parallel", "arbitrary"),
        name="deltanet",
    )(proj3, proj3, proj3, proj3, ba3, bat3, conv_w, conv_w, conv_w, arow, dtrow, acol, dtcol, gain2)


def _moba_prep_kernel(q_ref, k_ref, v_ref, qg_ref, kg_ref, qn_ref, kn_ref, vt_ref, km_ref):
    n = pl.program_id(1)
    scale = HEAD_DIM ** -0.5

    @pl.when(n == 0)
    def _():
        km_ref[...] = jnp.zeros_like(km_ref)

    row = lax.broadcasted_iota(jnp.int32, (km_ref.shape[1], HEAD_DIM), 0)
    ones = jnp.ones((MB_VROWS - HEAD_DIM, MB_BLOCK), BF16)
    lane = lax.broadcasted_iota(jnp.int32, (MB_BLOCK, LANES), 1)
    onehot = jnp.where(lane == n, 1.0, 0.0).astype(BF16)
    for h in range(HEADS):
        sl = slice(h * HEAD_DIM, (h + 1) * HEAD_DIM)
        ka = 2 * h * HEAD_DIM
        q = q_ref[0, :, sl].astype(F32)
        k = k_ref[0, :, sl].astype(F32)
        qn = q * lax.rsqrt(jnp.mean(q * q, axis=-1, keepdims=True) + NORM_EPS) * qg_ref[...]
        kn = k * lax.rsqrt(jnp.mean(k * k, axis=-1, keepdims=True) + NORM_EPS) * kg_ref[...]
        qn_ref[0, :, sl] = (qn * scale).astype(qn_ref.dtype)
        kn_ref[0, 0, :, ka:ka + HEAD_DIM] = kn.astype(BF16)
        kn_ref[0, 0, :, ka + HEAD_DIM:ka + 2 * HEAD_DIM] = onehot
        km_ref[0, :, sl] = jnp.where(row == n, jnp.mean(kn, axis=0, keepdims=True), km_ref[0, :, sl])
        vt_ref[0, 0, h, :HEAD_DIM, :] = v_ref[0, :, sl].astype(F32).T.astype(BF16)
        vt_ref[0, 0, h, HEAD_DIM:, :] = ones


def _moba_prep(proj3, q_gain, k_gain):
    b, t, _ = proj3.shape
    nb = t // MB_BLOCK
    w = HEADS * HEAD_DIM
    col = lambda r: pl.BlockSpec((1, MB_BLOCK, w), lambda bi, n, r=r: (bi, n, r))
    gspec = pl.BlockSpec((1, HEAD_DIM), lambda bi, n: (0, 0))
    return pl.pallas_call(
        _moba_prep_kernel,
        out_shape=(jax.ShapeDtypeStruct((b, t, w), BF16),
                   jax.ShapeDtypeStruct((b, nb, MB_BLOCK, 2 * w), BF16),
                   jax.ShapeDtypeStruct((b, nb, HEADS, MB_VROWS, MB_BLOCK), BF16),
                   jax.ShapeDtypeStruct((b, nb, w), F32)),
        grid=(b, nb),
        in_specs=[col(4), col(5), col(6), gspec, gspec],
        out_specs=(pl.BlockSpec((1, MB_BLOCK, w), lambda bi, n: (bi, n, 0)),
                   pl.BlockSpec((1, 1, MB_BLOCK, 2 * w), lambda bi, n: (bi, n, 0, 0)),
                   pl.BlockSpec((1, 1, HEADS, MB_VROWS, MB_BLOCK), lambda bi, n: (bi, n, 0, 0, 0)),
                   pl.BlockSpec((1, nb, w), lambda bi, n: (bi, 0, 0))),
        compiler_params=_params("parallel", "arbitrary"),
        name="moba_prep",
    )(proj3, proj3, proj3, q_gain.astype(F32).reshape(1, HEAD_DIM), k_gain.astype(F32).reshape(1, HEAD_DIM))


def _bucket_thresholds():
    max_exact = RP_BUCKETS // 2
    th = list(range(1, max_exact + 1))
    for kk in range(1, RP_BUCKETS - max_exact):
        x = max_exact * (RP_MAX_DIST / max_exact) ** (kk / (RP_BUCKETS - max_exact))
        r = round(x)
        th.append(r if abs(x - r) < 1e-9 else math.ceil(x))
    return th


N_NEAR = 5
LOG2E = 1.4426950408889634


def _moba_kernel(rb_ref, q_ref, k_ref, vt_ref, km_ref, o_ref, tab_ref, qa_ref, m_ref, alpha_ref, acc_ref, p_ref,
                 *, nb, hb):
    hg = pl.program_id(0)
    bi = pl.program_id(1)
    it = pl.program_id(2)
    blk = MB_BLOCK
    nq = MB_QBLOCKS
    tq = nq * blk
    shift = int(math.log2(blk))
    last = nq * it + nq - 1
    thresholds = _bucket_thresholds()
    heads = list(range(hb))
    hsl = lambda hh: slice(hh * HEAD_DIM, (hh + 1) * HEAD_DIM)

    @pl.when(jnp.logical_and(bi == 0, it == 0))
    def _():
        ko = lax.broadcasted_iota(jnp.int32, (blk, blk), 0)
        qo = lax.broadcasted_iota(jnp.int32, (blk, blk), 1)
        for hh in heads:
            h = hg * hb + hh
            far = rb_ref[RP_BUCKETS - 1, h]
            for delta in range(N_NEAR):
                d = delta * blk + qo - ko
                acc = jnp.full((blk, blk), rb_ref[0, h], F32)
                for bkt in range(1, RP_BUCKETS):
                    if delta * blk + blk - 1 >= thresholds[bkt - 1]:
                        acc = jnp.where(d >= thresholds[bkt - 1], rb_ref[bkt, h], acc)
                acc = (acc - far) * LOG2E
                if delta == 0:
                    acc = jnp.where(d >= 0, acc, NEG_BIG)
                tab_ref[hh, delta] = acc
            tab_ref[hh, N_NEAR] = jnp.zeros((blk, blk), F32)

    bidx = lax.broadcasted_iota(jnp.int32, (nb, tq), 0)
    qblk = nq * it + (lax.broadcasted_iota(jnp.int32, (nb, tq), 1) >> shift)
    past = bidx < qblk
    gates = []
    for hh in heads:
        kmh, kml = _split_bf16(km_ref[0, :, hsl(hh)])
        qb = q_ref[0, :, hsl(hh)]
        gate = (lax.dot_general(kmh, qb, NT_DIMS, preferred_element_type=F32)
                + lax.dot_general(kml, qb, NT_DIMS, preferred_element_type=F32))
        gates.append(jnp.where(past, gate, -jnp.inf))
    for hh, gate in zip(heads, gates):
        rank = jnp.zeros((nb, tq), jnp.int32)
        for jp in range(nb):
            gj = gate[jp:jp + 1, :]
            ahead = jnp.logical_or(gj > gate, jnp.logical_and(gj == gate, jp < bidx))
            rank = rank + ahead.astype(jnp.int32)
        allowed = jnp.logical_or(jnp.logical_and(past, rank < MB_TOPK), bidx == qblk)
        mask = jnp.where(allowed, 0.0, NEG_BIG)
        q_t = (q_ref[0, :, hsl(hh)].astype(F32) * LOG2E).T.astype(BF16)
        mask_rows = jnp.concatenate([mask, jnp.zeros((LANES - nb, tq), F32)], axis=0).astype(BF16)
        qa_ref[hh] = jnp.concatenate([q_t, mask_rows], axis=0)

    m_ref[...] = jnp.full(m_ref.shape, NEG_BIG, F32)
    acc_ref[...] = jnp.zeros(acc_ref.shape, F32)

    def logits(t, near, hs):
        j = last - t
        ss = [jnp.dot(k_ref[0, j, :, 2 * hh * HEAD_DIM:2 * (hh + 1) * HEAD_DIM], qa_ref[hh],
                      preferred_element_type=F32) for hh in hs]
        if near:
            idx = [jnp.clip(t - (nq - 1 - a), 0, N_NEAR) for a in range(nq)]
            ss = [jnp.concatenate([s[:, a * blk:(a + 1) * blk] + tab_ref[hh, idx[a]] for a in range(nq)], axis=1)
                  for hh, s in zip(hs, ss)]
        return ss

    def values(t, hs):
        slot = t & 1
        return [jnp.dot(vt_ref[0, last - t, hh], p_ref[slot, hh], preferred_element_type=F32) for hh in hs]

    def probabilities(t, ss, hs):
        slot = t & 1
        alphas = []
        for hh, s in zip(hs, ss):
            for q0 in range(0, tq, LANES):
                qs = slice(q0, q0 + LANES)
                m_old = m_ref[hh, :, qs]
                m_new = jnp.maximum(m_old, jnp.max(s[:, qs], axis=0, keepdims=True))
                m_ref[hh, :, qs] = m_new
                p_ref[slot, hh, :, qs] = jnp.exp2(s[:, qs] - m_new).astype(BF16)
                alphas.append((hh, qs, jnp.exp2(m_old - m_new)))
        return alphas

    def accumulate(pvs, hs):
        for hh, pv in zip(hs, pvs):
            acc_ref[hh] = alpha_ref[hh] * acc_ref[hh] + pv

    def set_alpha(alphas):
        for hh, qs, alpha in alphas:
            alpha_ref[hh, :, qs] = alpha

    groups = [heads[g:g + MB_ISSUE_GROUP] for g in range(0, hb, MB_ISSUE_GROUP)]

    def make_body(near):
        def body(t, carry):
            ahead = [logits(t, near, grp) for grp in groups[:MB_ISSUE_AHEAD]]
            for gi, grp in enumerate(groups):
                if gi + MB_ISSUE_AHEAD < len(groups):
                    ahead.append(logits(t, near, groups[gi + MB_ISSUE_AHEAD]))
                pvs = values(t - 1, grp)
                alphas = probabilities(t, ahead[gi], grp)
                accumulate(pvs, grp)
                set_alpha(alphas)
            return carry
        return body

    set_alpha(probabilities(0, logits(0, True, heads), heads))
    n_near = jnp.minimum(last + 1, N_NEAR + nq - 1)
    lax.fori_loop(1, n_near, make_body(True), 0)
    lax.fori_loop(n_near, last + 1, make_body(False), 0)
    accumulate(values(last, heads), heads)
    for hh in heads:
        acc = acc_ref[hh]
        o_ref[0, :, hsl(hh)] = (acc[:HEAD_DIM] / acc[HEAD_DIM:HEAD_DIM + 1]).T.astype(o_ref.dtype)


def _moba(qn, kn, vt, kmean, rel_bias):
    b, t, w = qn.shape
    nb = t // MB_BLOCK
    hb = MB_HEAD_BLOCK
    hw = hb * HEAD_DIM
    tq = MB_QBLOCKS * MB_BLOCK
    return pl.pallas_call(
        functools.partial(_moba_kernel, nb=nb, hb=hb),
        out_shape=jax.ShapeDtypeStruct((b, t, w), BF16),
        grid=(HEADS // hb, b, t // tq),
        in_specs=[pl.BlockSpec(memory_space=pltpu.SMEM),
                  pl.BlockSpec((1, tq, hw), lambda g, bi, i: (bi, i, g)),
                  pl.BlockSpec((1, nb, MB_BLOCK, 2 * hw), lambda g, bi, i: (bi, 0, 0, g),
                               pipeline_mode=pl.Buffered(1)),
                  pl.BlockSpec((1, nb, hb, MB_VROWS, MB_BLOCK), lambda g, bi, i: (bi, 0, g, 0, 0),
                               pipeline_mode=pl.Buffered(1)),
                  pl.BlockSpec((1, nb, hw), lambda g, bi, i: (bi, 0, g))],
        out_specs=pl.BlockSpec((1, tq, hw), lambda g, bi, i: (bi, i, g)),
        scratch_shapes=[pltpu.VMEM((hb, N_NEAR + 1, MB_BLOCK, MB_BLOCK), F32),
                        pltpu.VMEM((hb, 2 * HEAD_DIM, tq), BF16),
                        pltpu.VMEM((hb, 1, tq), F32),
                        pltpu.VMEM((hb, 1, tq), F32),
                        pltpu.VMEM((hb, MB_VROWS, tq), F32),
                        pltpu.VMEM((2, hb, MB_BLOCK, tq), BF16)],
        compiler_params=_params("arbitrary", "arbitrary", "arbitrary"),
        name="moba_attn",
    )(rel_bias.astype(F32), qn, kn, vt, kmean)


def _merge_kernel(ya_ref, yb_ref, wa_ref, wb_ref, ga_ref, gb_ref, o_ref):
    pa = jnp.dot(ya_ref[...], wa_ref[...], preferred_element_type=F32)
    pb = jnp.dot(yb_ref[...], wb_ref[...], preferred_element_type=F32)
    ga = ga_ref[...].astype(F32)
    gb = gb_ref[...].astype(F32)
    o_ref[...] = (jax.nn.sigmoid(ga) * pa + jax.nn.sigmoid(gb) * pb).astype(o_ref.dtype)


def _merge(ya, yb, wa, wb, proj, gate_col):
    m, d = ya.shape
    tm, tn = 1024, 1024
    gpb = d // tn
    return pl.pallas_call(
        _merge_kernel,
        out_shape=jax.ShapeDtypeStruct((m, d), BF16),
        grid=(m // tm, d // tn),
        in_specs=[pl.BlockSpec((tm, d), lambda i, j: (i, 0)),
                  pl.BlockSpec((tm, d), lambda i, j: (i, 0)),
                  pl.BlockSpec((d, tn), lambda i, j: (0, j)),
                  pl.BlockSpec((d, tn), lambda i, j: (0, j)),
                  pl.BlockSpec((tm, tn), lambda i, j: (i, gate_col * gpb + j)),
                  pl.BlockSpec((tm, tn), lambda i, j: (i, (gate_col + 1) * gpb + j))],
        out_specs=pl.BlockSpec((tm, tn), lambda i, j: (i, j)),
        compiler_params=_params("parallel", "arbitrary"),
        name="merge",
    )(ya, yb, wa, wb, proj, proj)


def kernel(x, c, ada_w, ada_b, norm1_g, ffn1_w1, ffn1_w3, ffn1_w2, norm2_g, w_in, dn_conv_w, dn_a_log, dn_dt_bias, dn_norm_g, mb_q_norm_g, mb_k_norm_g, rel_bias, w_proj_a, w_proj_b, w_out, norm3_g, ffn2_w1, ffn2_w3, ffn2_w2):
    b, t, d = x.shape
    m = b * t
    depth = ada_w.shape[0]
    wide = HEADS * HEAD_DIM
    cut0 = 4 * wide
    cut1 = cut0 + 2 * HEADS
    h = x.reshape(m, d)
    c_pad = jnp.zeros((8, d), F32).at[:b].set(c)
    for l in range(depth):
        ada = _ada(c_pad, ada_w, ada_b.reshape(depth, 1, -1), l)[:b]
        ada3 = ada.reshape(b, N_ADA, d)
        h = _ffn(h, ada3, norm1_g[l].reshape(1, d), ffn1_w1[l].astype(BF16), ffn1_w3[l].astype(BF16),
                 ffn1_w2[l].astype(BF16), slot=0, seq=t)

        w_in16 = w_in.astype(BF16)
        w_tail = w_in16[l][:, cut1:]
        w_ba = jnp.zeros((d, LANES), BF16).at[:, :2 * HEADS].set(w_in16[l][:, cut0:cut1])
        proj, ba = _inproj(h, ada3, norm2_g[l].reshape(1, d), w_in16, l, cut0, w_tail, w_ba, seq=t)
        proj3 = proj.reshape(b, t, -1)
        ba3 = ba.reshape(b, t, LANES)
        bat3 = jnp.transpose(ba3, (0, 2, 1))
        y_a = _deltanet(proj3, ba3, bat3, dn_conv_w[l], dn_a_log[l], dn_dt_bias[l], dn_norm_g[l])
        qn, kn, vt, kmean = _moba_prep(proj3, mb_q_norm_g[l], mb_k_norm_g[l])
        y_b = _moba(qn, kn, vt, kmean, rel_bias)
        merged = _merge(y_a.reshape(m, wide), y_b.reshape(m, wide), w_proj_a[l].astype(BF16),
                        w_proj_b[l].astype(BF16), proj, 7)
        h = _ffn(h, ada3, norm3_g[l].reshape(1, d), ffn2_w1[l].astype(BF16), ffn2_w3[l].astype(BF16),
                 ffn2_w2[l].astype(BF16), slot=6, seq=t, mix_x=merged, mix_w=w_out[l].astype(BF16))
    return h.reshape(b, t, d)
```

```python
import functools
import math

import jax
import jax.numpy as jnp
from jax import lax
from jax.experimental import pallas as pl
from jax.experimental.pallas import tpu as pltpu

F32 = jnp.float32
BF16 = jnp.bfloat16

LANES = 128
SUBLANES = 8
NORM_EPS = 1e-6
N_ADA = 9
HEADS = 16
HEAD_DIM = 128
DN_CONV = 4
DN_CHUNK = 128
DN_HEAD_BLOCK = 16
DN_INV_BASE = 8
MB_BLOCK = 256
MB_TOPK = 3
MB_HEAD_BLOCK = 8
MB_QBLOCKS = 1
MB_ISSUE_GROUP = 1
MB_ISSUE_AHEAD = 2
MB_ISSUE_AHEAD_NEAR = 1
MB_VROWS = HEAD_DIM + 16
RP_BUCKETS = 32
RP_MAX_DIST = 1024
NEG_BIG = -1e30
VMEM_LIMIT = 56 * 1024 * 1024

NT_DIMS = (((1,), (1,)), ((), ()))
NN_DIMS = (((1,), (0,)), ((), ()))
TN_DIMS = (((0,), (0,)), ((), ()))


def _dot(a, b, dims=NN_DIMS):
    return lax.dot_general(a.astype(BF16), b.astype(BF16), dims, preferred_element_type=F32)


def _split_bf16(x):
    hi = x.astype(BF16)
    lo = (x - hi.astype(F32)).astype(BF16)
    return hi, lo


def _dot3(a, b, dims=NN_DIMS):
    ah, al = _split_bf16(a)
    bh, bl = _split_bf16(b)
    f = lambda x, y: lax.dot_general(x, y, dims, preferred_element_type=F32)
    return f(ah, bh) + (f(ah, bl) + f(al, bh))


def _silu(x):
    return x * jax.nn.sigmoid(x)


def _params(*sem):
    return pltpu.CompilerParams(dimension_semantics=sem, vmem_limit_bytes=VMEM_LIMIT)


def _ada_kernel(c_ref, w_ref, b_ref, o_ref):
    s = _silu(c_ref[...])
    o_ref[...] = _dot3(s, w_ref[0]) + b_ref[0]


def _ada(c_pad, w, b, layer):
    rows, d = c_pad.shape
    n = w.shape[2]
    tn = 1024
    return pl.pallas_call(
        _ada_kernel,
        out_shape=jax.ShapeDtypeStruct((rows, n), F32),
        grid=(n // tn,),
        in_specs=[pl.BlockSpec((rows, d), lambda j: (0, 0)),
                  pl.BlockSpec((1, d, tn), lambda j: (layer, 0, j)),
                  pl.BlockSpec((1, 1, tn), lambda j: (layer, 0, j))],
        out_specs=pl.BlockSpec((rows, tn), lambda j: (0, j)),
        compiler_params=_params("arbitrary"),
        name="ada_proj",
    )(c_pad, w, b)


def _norm_modulate(h, gain, shift, scale):
    ms = jnp.mean(h * h, axis=-1, keepdims=True)
    return (h * lax.rsqrt(ms + NORM_EPS)) * (gain * (1.0 + scale)) + shift


def _ffn_kernel(*refs, slot, nf, mixer):
    if mixer:
        h_ref, x_ref, wm_ref, ada_ref, g_ref, w1_ref, w3_ref, w2_ref, o_ref, u_ref, acc_ref = refs
    else:
        h_ref, ada_ref, g_ref, w1_ref, w3_ref, w2_ref, o_ref, u_ref, acc_ref = refs
    f = pl.program_id(1)

    @pl.when(f == 0)
    def _():
        r = h_ref[...]
        if mixer:
            r = r + ada_ref[0, slot - 1:slot, :] * jnp.dot(x_ref[...], wm_ref[...], preferred_element_type=F32)
            o_ref[...] = r
        u = _norm_modulate(r, g_ref[...], ada_ref[0, slot:slot + 1, :], ada_ref[0, slot + 1:slot + 2, :])
        u_ref[...] = u.astype(BF16)
        acc_ref[...] = jnp.zeros_like(acc_ref)

    u = u_ref[...]
    a = jnp.dot(u, w1_ref[...], preferred_element_type=F32)
    b = jnp.dot(u, w3_ref[...], preferred_element_type=F32)
    acc_ref[...] += jnp.dot((_silu(a) * b).astype(BF16), w2_ref[...], preferred_element_type=F32)

    @pl.when(f == nf - 1)
    def _():
        r = o_ref[...] if mixer else h_ref[...]
        o_ref[...] = r + 0.5 * ada_ref[0, slot + 2:slot + 3, :] * acc_ref[...]


def _ffn(h, ada3, gain, w1, w3, w2, *, slot, seq, mix_x=None, mix_w=None):
    m, d = h.shape
    dff = w1.shape[1]
    tm, tf = 512, 512
    nf = dff // tf
    per_batch = seq // tm
    mixer = mix_x is not None
    row = lambda shape: pl.BlockSpec(shape, lambda i, f: (i, 0))
    mix_specs = [row((tm, mix_x.shape[1])),
                 pl.BlockSpec(mix_w.shape, lambda i, f: (0, 0), pipeline_mode=pl.Buffered(1))] if mixer else []
    return pl.pallas_call(
        functools.partial(_ffn_kernel, slot=slot, nf=nf, mixer=mixer),
        out_shape=jax.ShapeDtypeStruct((m, d), F32),
        grid=(m // tm, nf),
        in_specs=[row((tm, d))] + mix_specs + [
                  pl.BlockSpec((1, N_ADA, d), lambda i, f: (i // per_batch, 0, 0)),
                  pl.BlockSpec((1, d), lambda i, f: (0, 0)),
                  pl.BlockSpec((d, tf), lambda i, f: (0, f)),
                  pl.BlockSpec((d, tf), lambda i, f: (0, f)),
                  pl.BlockSpec((tf, d), lambda i, f: (f, 0))],
        out_specs=row((tm, d)),
        scratch_shapes=[pltpu.VMEM((tm, d), BF16), pltpu.VMEM((tm, d), F32)],
        compiler_params=_params("parallel", "arbitrary"),
        name="ffn",
    )(*([h] + ([mix_x, mix_w] if mixer else []) + [ada3, gain, w1, w3, w2]))


def _inproj_kernel(h_ref, ada_ref, g_ref, wa_ref, wb_ref, wba_ref, o_ref, ba_ref, u_ref, *, na):
    j = pl.program_id(1)

    @pl.when(j == 0)
    def _():
        u = _norm_modulate(h_ref[...], g_ref[...], ada_ref[0, 3:4, :], ada_ref[0, 4:5, :])
        u_ref[...] = u.astype(BF16)
        ba_ref[...] = jnp.dot(u_ref[...], wba_ref[...], preferred_element_type=F32)

    @pl.when(j < na)
    def _():
        o_ref[...] = jnp.dot(u_ref[...], wa_ref[0], preferred_element_type=F32).astype(o_ref.dtype)

    @pl.when(j >= na)
    def _():
        o_ref[...] = jnp.dot(u_ref[...], wb_ref[...], preferred_element_type=F32).astype(o_ref.dtype)


def _inproj(h, ada3, gain, w_all, layer, n_lead, wb, wba, *, seq):
    m, d = h.shape
    tm, tn = 1024, 1024
    na = n_lead // tn
    n = n_lead + wb.shape[1]
    per_batch = seq // tm
    return pl.pallas_call(
        functools.partial(_inproj_kernel, na=na),
        out_shape=(jax.ShapeDtypeStruct((m, n), BF16), jax.ShapeDtypeStruct((m, LANES), F32)),
        grid=(m // tm, n // tn),
        in_specs=[pl.BlockSpec((tm, d), lambda i, j: (i, 0)),
                  pl.BlockSpec((1, N_ADA, d), lambda i, j: (i // per_batch, 0, 0)),
                  pl.BlockSpec((1, d), lambda i, j: (0, 0)),
                  pl.BlockSpec((1, d, tn), lambda i, j: (layer, 0, jnp.minimum(j, na - 1))),
                  pl.BlockSpec((d, tn), lambda i, j: (0, jnp.maximum(j - na, 0))),
                  pl.BlockSpec((d, LANES), lambda i, j: (0, 0))],
        out_specs=(pl.BlockSpec((tm, tn), lambda i, j: (i, j)),
                   pl.BlockSpec((tm, LANES), lambda i, j: (i, 0))),
        scratch_shapes=[pltpu.VMEM((tm, d), BF16)],
        compiler_params=_params("parallel", "arbitrary"),
        name="in_proj",
    )(h, ada3, gain, w_all, wb, wba)


def _softplus(x):
    return jnp.maximum(x, 0.0) + jnp.log1p(jnp.exp(-jnp.abs(x)))


def _dn_kernel(q_ref, k_ref, v_ref, z_ref, ba_ref, bat_ref, wq_ref, wk_ref, wv_ref,
               arow_ref, dtrow_ref, acol_ref, dtcol_ref, gain_ref, o_ref,
               prev_ref, state_ref, gcrow_ref, *, hb):
    c = DN_CHUNK
    hg = pl.program_id(1)
    n = pl.program_id(2)

    @pl.when(n == 0)
    def _():
        prev_ref[...] = jnp.zeros_like(prev_ref)
        state_ref[...] = jnp.zeros_like(state_ref)

    width = hb * HEAD_DIM
    sub = prev_ref.shape[1]
    row_t = lax.broadcasted_iota(jnp.int32, (sub, width), 0)

    def conv_silu(x_ref, w_ref, s):
        cur = x_ref[0].astype(F32)
        tail = prev_ref[s]
        acc = cur * w_ref[DN_CONV - 1:DN_CONV, :]
        for d in range(1, DN_CONV):
            shifted = pltpu.roll(cur, d, 0)
            head = jnp.where(row_t < d, pltpu.roll(tail, d, 0), shifted[:sub])
            shifted = jnp.concatenate([head, shifted[sub:]], axis=0)
            acc = acc + shifted * w_ref[DN_CONV - 1 - d:DN_CONV - d, :]
        prev_ref[s] = cur[c - sub:]
        return _silu(acc)

    xq = conv_silu(q_ref, wq_ref, 0)
    xk = conv_silu(k_ref, wk_ref, 1)
    xv = conv_silu(v_ref, wv_ref, 2)

    slab = ba_ref[0]
    beta_mat = jax.nn.sigmoid(slab)
    g_mat = -jnp.exp(arow_ref[...]) * _softplus(slab + dtrow_ref[...])
    ri = lax.broadcasted_iota(jnp.int32, (c, c), 0)
    ci = lax.broadcasted_iota(jnp.int32, (c, c), 1)
    lower_incl = (ri >= ci)
    tri_l = jnp.where(lower_incl, 1.0, 0.0).astype(BF16)
    tri_u = jnp.where(ri <= ci, 1.0, 0.0).astype(BF16)

    def exact_ones_dot(x, ones, left):
        x1 = x.astype(BF16)
        r1 = x - x1.astype(F32)
        x2 = r1.astype(BF16)
        x3 = (r1 - x2.astype(F32)).astype(BF16)
        if left:
            f = lambda p: jnp.dot(ones, p, preferred_element_type=F32)
        else:
            f = lambda p: jnp.dot(p, ones, preferred_element_type=F32)
        return f(x1) + (f(x2) + f(x3))

    gc_colmat = exact_ones_dot(g_mat, tri_l, True)
    a_rows = bat_ref[0, HEADS:2 * HEADS, :]
    g_rows = -jnp.exp(acol_ref[...]) * _softplus(a_rows + dtcol_ref[...])
    gcrow_ref[...] = exact_ones_dot(g_rows, tri_u, False)

    lane = lax.broadcasted_iota(jnp.int32, (c, LANES), 1)
    eye = jnp.where(ri == ci, 1.0, 0.0)
    strict = ri > ci
    scale = HEAD_DIM ** -0.5

    hs = []
    for hh in range(hb):
        h = hg * hb + hh
        sl = slice(hh * HEAD_DIM, (hh + 1) * HEAD_DIM)
        beta = jnp.sum(jnp.where(lane == h, beta_mat, 0.0), axis=1, keepdims=True)
        gcc = jnp.sum(jnp.where(lane == HEADS + h, gc_colmat, 0.0), axis=1, keepdims=True)
        gcr = gcrow_ref[pl.ds(h, 1), :]
        gc_last = gcr[:, c - 1:c]
        decay = jnp.where(lower_incl, jnp.exp(jnp.minimum(gcc - gcr, 0.0)), 0.0)
        q = xq[:, sl]
        k = xk[:, sl]
        q = q * (lax.rsqrt(jnp.sum(q * q, axis=-1, keepdims=True) + NORM_EPS) * scale)
        k = k * lax.rsqrt(jnp.sum(k * k, axis=-1, keepdims=True) + NORM_EPS)
        kb = k * beta
        egc = jnp.exp(gcc)
        hs.append(dict(hh=hh, sl=sl, decay=decay, q=q, k=k, kb=kb,
                       rhs=jnp.concatenate([xv[:, sl] * beta, kb * egc], axis=1),
                       qd=q * egc, kd=k * jnp.exp(gc_last - gcc), g_tot=jnp.exp(gc_last)))

    for s in hs:
        s["lmat"] = jnp.where(strict, _dot(s["kb"], s["k"], NT_DIMS) * s["decay"], 0.0)
        s["qk"] = _dot(s["q"], s["k"], NT_DIMS) * s["decay"]
    shift0 = int(math.log2(DN_INV_BASE))
    mm = lambda a, b: jnp.dot(a, b, preferred_element_type=F32)
    same_block = jnp.where((ri >> shift0) == (ci >> shift0), 1.0, 0.0).astype(BF16)
    for s in hs:
        s["lb"] = s["lmat"].astype(BF16)
        s["pw"] = -(s["lb"] * same_block)
        s["tmat"] = eye + s["pw"].astype(F32)
    for _ in range(shift0 - 1):
        for s in hs:
            s["pw"] = mm(s["pw"], s["pw"]).astype(BF16)
        for s in hs:
            s["tmat"] = s["tmat"] + mm(s["tmat"].astype(BF16), s["pw"])
    for sh in range(shift0, int(math.log2(c))):
        lower_left = jnp.where(jnp.logical_and((ri >> sh) == (ci >> sh) + 1, ((ci >> sh) & 1) == 0),
                               1.0, 0.0).astype(BF16)
        for s in hs:
            s["tb"] = s["tmat"].astype(BF16)
            s["pw"] = mm(s["lb"] * lower_left, s["tb"]).astype(BF16)
        for s in hs:
            s["tmat"] = s["tmat"] - mm(s["tb"], s["pw"])
    for s in hs:
        s["uw"] = _dot(s["tmat"], s["rhs"])
    for s in hs:
        s["state"] = state_ref[s["hh"]]
        s["v_new"] = s["uw"][:, :HEAD_DIM] - _dot(s["uw"][:, HEAD_DIM:], s["state"])
    for s in hs:
        s["o"] = _dot(s["qd"], s["state"]) + _dot(s["qk"], s["v_new"])
        state_ref[s["hh"]] = s["state"] * s["g_tot"] + _dot(s["kd"], s["v_new"], TN_DIMS)
    for s in hs:
        o = s["o"]
        o = o * lax.rsqrt(jnp.mean(o * o, axis=-1, keepdims=True) + NORM_EPS) * gain_ref[...]
        o_ref[0, :, s["sl"]] = (o * _silu(z_ref[0, :, s["sl"]].astype(F32))).astype(o_ref.dtype)


def _deltanet(proj3, ba3, bat3, conv_w, a_log, dt_bias, gain):
    b, t, _ = proj3.shape
    hb = DN_HEAD_BLOCK
    c = DN_CHUNK
    width = hb * HEAD_DIM
    ng = HEADS // hb
    region = (HEADS * HEAD_DIM) // width

    def pad_row(x):
        return jnp.zeros((1, LANES), F32).at[0, HEADS:2 * HEADS].set(x.astype(F32))

    arow, dtrow = pad_row(a_log), pad_row(dt_bias)
    acol = a_log.astype(F32).reshape(HEADS, 1)
    dtcol = dt_bias.astype(F32).reshape(HEADS, 1)
    gain2 = gain.astype(F32).reshape(1, HEAD_DIM)

    def col(r):
        return pl.BlockSpec((1, c, width), lambda bi, g, n, r=r: (bi, n, r * region + g))

    def wcol(r):
        return pl.BlockSpec((DN_CONV, width), lambda bi, g, n, r=r: (0, r * region + g))

    const = lambda shape: pl.BlockSpec(shape, lambda bi, g, n: tuple(0 for _ in shape))
    return pl.pallas_call(
        functools.partial(_dn_kernel, hb=hb),
        out_shape=jax.ShapeDtypeStruct((b, t, HEADS * HEAD_DIM), BF16),
        grid=(b, ng, t // c),
        in_specs=[col(0), col(1), col(2), col(3),
                  pl.BlockSpec((1, c, LANES), lambda bi, g, n: (bi, n, 0)),
                  pl.BlockSpec((1, LANES, c), lambda bi, g, n: (bi, 0, n)),
                  wcol(0), wcol(1), wcol(2),
                  const((1, LANES)), const((1, LANES)), const((HEADS, 1)), const((HEADS, 1)),
                  const((1, HEAD_DIM))],
        out_specs=pl.BlockSpec((1, c, width), lambda bi, g, n: (bi, n, g)),
        scratch_shapes=[pltpu.VMEM((3, SUBLANES, width), F32),
                        pltpu.VMEM((hb, HEAD_DIM, HEAD_DIM), F32),
                        pltpu.VMEM((HEADS, c), F32)],
        compiler_params=_params("parallel", "parallel", "arbitrary"),
        name="deltanet",
    )(proj3, proj3, proj3, proj3, ba3, bat3, conv_w, conv_w, conv_w, arow, dtrow, acol, dtcol, gain2)


def _moba_prep_kernel(q_ref, k_ref, v_ref, qg_ref, kg_ref, qn_ref, kn_ref, vt_ref, km_ref):
    n = pl.program_id(1)
    scale = HEAD_DIM ** -0.5

    @pl.when(n == 0)
    def _():
        km_ref[...] = jnp.zeros_like(km_ref)

    row = lax.broadcasted_iota(jnp.int32, (km_ref.shape[1], HEAD_DIM), 0)
    ones = jnp.ones((MB_VROWS - HEAD_DIM, MB_BLOCK), BF16)
    lane = lax.broadcasted_iota(jnp.int32, (MB_BLOCK, LANES), 1)
    onehot = jnp.where(lane == n, 1.0, 0.0).astype(BF16)
    for h in range(HEADS):
        sl = slice(h * HEAD_DIM, (h + 1) * HEAD_DIM)
        ka = 2 * h * HEAD_DIM
        q = q_ref[0, :, sl].astype(F32)
        k = k_ref[0, :, sl].astype(F32)
        qn = q * lax.rsqrt(jnp.mean(q * q, axis=-1, keepdims=True) + NORM_EPS) * qg_ref[...]
        kn = k * lax.rsqrt(jnp.mean(k * k, axis=-1, keepdims=True) + NORM_EPS) * kg_ref[...]
        qn_ref[0, :, sl] = (qn * scale).astype(qn_ref.dtype)
        kn_ref[0, 0, :, ka:ka + HEAD_DIM] = kn.astype(BF16)
        kn_ref[0, 0, :, ka + HEAD_DIM:ka + 2 * HEAD_DIM] = onehot
        km_ref[0, :, sl] = jnp.where(row == n, jnp.mean(kn, axis=0, keepdims=True), km_ref[0, :, sl])
        vt_ref[0, 0, h, :HEAD_DIM, :] = v_ref[0, :, sl].astype(F32).T.astype(BF16)
        vt_ref[0, 0, h, HEAD_DIM:, :] = ones


def _moba_prep(proj3, q_gain, k_gain):
    b, t, _ = proj3.shape
    nb = t // MB_BLOCK
    w = HEADS * HEAD_DIM
    col = lambda r: pl.BlockSpec((1, MB_BLOCK, w), lambda bi, n, r=r: (bi, n, r))
    gspec = pl.BlockSpec((1, HEAD_DIM), lambda bi, n: (0, 0))
    return pl.pallas_call(
        _moba_prep_kernel,
        out_shape=(jax.ShapeDtypeStruct((b, t, w), BF16),
                   jax.ShapeDtypeStruct((b, nb, MB_BLOCK, 2 * w), BF16),
                   jax.ShapeDtypeStruct((b, nb, HEADS, MB_VROWS, MB_BLOCK), BF16),
                   jax.ShapeDtypeStruct((b, nb, w), F32)),
        grid=(b, nb),
        in_specs=[col(4), col(5), col(6), gspec, gspec],
        out_specs=(pl.BlockSpec((1, MB_BLOCK, w), lambda bi, n: (bi, n, 0)),
                   pl.BlockSpec((1, 1, MB_BLOCK, 2 * w), lambda bi, n: (bi, n, 0, 0)),
                   pl.BlockSpec((1, 1, HEADS, MB_VROWS, MB_BLOCK), lambda bi, n: (bi, n, 0, 0, 0)),
                   pl.BlockSpec((1, nb, w), lambda bi, n: (bi, 0, 0))),
        compiler_params=_params("parallel", "arbitrary"),
        name="moba_prep",
    )(proj3, proj3, proj3, q_gain.astype(F32).reshape(1, HEAD_DIM), k_gain.astype(F32).reshape(1, HEAD_DIM))


def _bucket_thresholds():
    max_exact = RP_BUCKETS // 2
    th = list(range(1, max_exact + 1))
    for kk in range(1, RP_BUCKETS - max_exact):
        x = max_exact * (RP_MAX_DIST / max_exact) ** (kk / (RP_BUCKETS - max_exact))
        r = round(x)
        th.append(r if abs(x - r) < 1e-9 else math.ceil(x))
    return th


N_NEAR = 5
LOG2E = 1.4426950408889634


def _moba_kernel(rb_ref, q_ref, k_ref, vt_ref, km_ref, o_ref, tab_ref, qa_ref, m_ref, alpha_ref, acc_ref, p_ref,
                 *, nb, hb):
    hg = pl.program_id(0)
    bi = pl.program_id(1)
    it = pl.program_id(2)
    blk = MB_BLOCK
    nq = MB_QBLOCKS
    tq = nq * blk
    shift = int(math.log2(blk))
    last = nq * it + nq - 1
    thresholds = _bucket_thresholds()
    heads = list(range(hb))
    hsl = lambda hh: slice(hh * HEAD_DIM, (hh + 1) * HEAD_DIM)

    @pl.when(jnp.logical_and(bi == 0, it == 0))
    def _():
        ko = lax.broadcasted_iota(jnp.int32, (blk, blk), 0)
        qo = lax.broadcasted_iota(jnp.int32, (blk, blk), 1)
        for hh in heads:
            h = hg * hb + hh
            far = rb_ref[RP_BUCKETS - 1, h]
            for delta in range(N_NEAR):
                d = delta * blk + qo - ko
                acc = jnp.full((blk, blk), rb_ref[0, h], F32)
                for bkt in range(1, RP_BUCKETS):
                    if delta * blk + blk - 1 >= thresholds[bkt - 1]:
                        acc = jnp.where(d >= thresholds[bkt - 1], rb_ref[bkt, h], acc)
                acc = (acc - far) * LOG2E
                if delta == 0:
                    acc = jnp.where(d >= 0, acc, NEG_BIG)
                tab_ref[hh, delta] = acc
            tab_ref[hh, N_NEAR] = jnp.zeros((blk, blk), F32)

    bidx = lax.broadcasted_iota(jnp.int32, (nb, tq), 0)
    qblk = nq * it + (lax.broadcasted_iota(jnp.int32, (nb, tq), 1) >> shift)
    past = bidx < qblk
    gates = []
    for hh in heads:
        kmh, kml = _split_bf16(km_ref[0, :, hsl(hh)])
        qb = q_ref[0, :, hsl(hh)]
        gate = (lax.dot_general(kmh, qb, NT_DIMS, preferred_element_type=F32)
                + lax.dot_general(kml, qb, NT_DIMS, preferred_element_type=F32))
        gates.append(jnp.where(past, gate, -jnp.inf))
    for hh, gate in zip(heads, gates):
        rank = jnp.zeros((nb, tq), jnp.int32)
        for jp in range(nb):
            gj = gate[jp:jp + 1, :]
            ahead = jnp.logical_or(gj > gate, jnp.logical_and(gj == gate, jp < bidx))
            rank = rank + ahead.astype(jnp.int32)
        allowed = jnp.logical_or(jnp.logical_and(past, rank < MB_TOPK), bidx == qblk)
        mask = jnp.where(allowed, 0.0, NEG_BIG)
        q_t = (q_ref[0, :, hsl(hh)].astype(F32) * LOG2E).T.astype(BF16)
        mask_rows = jnp.concatenate([mask, jnp.zeros((LANES - nb, tq), F32)], axis=0).astype(BF16)
        qa_ref[hh] = jnp.concatenate([q_t, mask_rows], axis=0)

    m_ref[...] = jnp.full(m_ref.shape, NEG_BIG, F32)
    acc_ref[...] = jnp.zeros(acc_ref.shape, F32)

    def logits(t, near, hs):
        j = last - t
        ss = [jnp.dot(k_ref[0, j, :, 2 * hh * HEAD_DIM:2 * (hh + 1) * HEAD_DIM], qa_ref[hh],
                      preferred_element_type=F32) for hh in hs]
        if near:
            idx = [jnp.clip(t - (nq - 1 - a), 0, N_NEAR) for a in range(nq)]
            ss = [jnp.concatenate([s[:, a * blk:(a + 1) * blk] + tab_ref[hh, idx[a]] for a in range(nq)], axis=1)
                  for hh, s in zip(hs, ss)]
        return ss

    def values(t, hs):
        slot = t & 1
        return [jnp.dot(vt_ref[0, last - t, hh], p_ref[slot, hh], preferred_element_type=F32) for hh in hs]

    def probabilities(t, ss, hs):
        slot = t & 1
        alphas = []
        for hh, s in zip(hs, ss):
            for q0 in range(0, tq, LANES):
                qs = slice(q0, q0 + LANES)
                m_old = m_ref[hh, :, qs]
                m_new = jnp.maximum(m_old, jnp.max(s[:, qs], axis=0, keepdims=True))
                m_ref[hh, :, qs] = m_new
                p_ref[slot, hh, :, qs] = jnp.exp2(s[:, qs] - m_new).astype(BF16)
                alphas.append((hh, qs, jnp.exp2(m_old - m_new)))
        return alphas

    def accumulate(pvs, hs):
        for hh, pv in zip(hs, pvs):
            acc_ref[hh] = alpha_ref[hh] * acc_ref[hh] + pv

    def set_alpha(alphas):
        for hh, qs, alpha in alphas:
            alpha_ref[hh, :, qs] = alpha

    groups = [heads[g:g + MB_ISSUE_GROUP] for g in range(0, hb, MB_ISSUE_GROUP)]

    def make_body(near):
        n_ahead = MB_ISSUE_AHEAD_NEAR if near else MB_ISSUE_AHEAD

        def body(t, carry):
            ahead = [logits(t, near, grp) for grp in groups[:n_ahead]]
            for gi, grp in enumerate(groups):
                if gi + n_ahead < len(groups):
                    ahead.append(logits(t, near, groups[gi + n_ahead]))
                pvs = values(t - 1, grp)
                alphas = probabilities(t, ahead[gi], grp)
                accumulate(pvs, grp)
                set_alpha(alphas)
            return carry
        return body

    set_alpha(probabilities(0, logits(0, True, heads), heads))
    n_near = jnp.minimum(last + 1, N_NEAR + nq - 1)
    lax.fori_loop(1, n_near, make_body(True), 0)
    lax.fori_loop(n_near, last + 1, make_body(False), 0)
    accumulate(values(last, heads), heads)
    for hh in heads:
        acc = acc_ref[hh]
        o_ref[0, :, hsl(hh)] = (acc[:HEAD_DIM] / acc[HEAD_DIM:HEAD_DIM + 1]).T.astype(o_ref.dtype)


def _moba(qn, kn, vt, kmean, rel_bias):
    b, t, w = qn.shape
    nb = t // MB_BLOCK
    hb = MB_HEAD_BLOCK
    hw = hb * HEAD_DIM
    tq = MB_QBLOCKS * MB_BLOCK
    return pl.pallas_call(
        functools.partial(_moba_kernel, nb=nb, hb=hb),
        out_shape=jax.ShapeDtypeStruct((b, t, w), BF16),
        grid=(HEADS // hb, b, t // tq),
        in_specs=[pl.BlockSpec(memory_space=pltpu.SMEM),
                  pl.BlockSpec((1, tq, hw), lambda g, bi, i: (bi, i, g)),
                  pl.BlockSpec((1, nb, MB_BLOCK, 2 * hw), lambda g, bi, i: (bi, 0, 0, g),
                               pipeline_mode=pl.Buffered(1)),
                  pl.BlockSpec((1, nb, hb, MB_VROWS, MB_BLOCK), lambda g, bi, i: (bi, 0, g, 0, 0),
                               pipeline_mode=pl.Buffered(1)),
                  pl.BlockSpec((1, nb, hw), lambda g, bi, i: (bi, 0, g))],
        out_specs=pl.BlockSpec((1, tq, hw), lambda g, bi, i: (bi, i, g)),
        scratch_shapes=[pltpu.VMEM((hb, N_NEAR + 1, MB_BLOCK, MB_BLOCK), F32),
                        pltpu.VMEM((hb, 2 * HEAD_DIM, tq), BF16),
                        pltpu.VMEM((hb, 1, tq), F32),
                        pltpu.VMEM((hb, 1, tq), F32),
                        pltpu.VMEM((hb, MB_VROWS, tq), F32),
                        pltpu.VMEM((2, hb, MB_BLOCK, tq), BF16)],
        compiler_params=_params("arbitrary", "arbitrary", "arbitrary"),
        name="moba_attn",
    )(rel_bias.astype(F32), qn, kn, vt, kmean)


def _merge_kernel(ya_ref, yb_ref, wa_ref, wb_ref, ga_ref, gb_ref, o_ref):
    pa = jnp.dot(ya_ref[...], wa_ref[...], preferred_element_type=F32)
    pb = jnp.dot(yb_ref[...], wb_ref[...], preferred_element_type=F32)
    ga = ga_ref[...].astype(F32)
    gb = gb_ref[...].astype(F32)
    o_ref[...] = (jax.nn.sigmoid(ga) * pa + jax.nn.sigmoid(gb) * pb).astype(o_ref.dtype)


def _merge(ya, yb, wa, wb, proj, gate_col):
    m, d = ya.shape
    tm, tn = 512, d
    gpb = d // tn
    return pl.pallas_call(
        _merge_kernel,
        out_shape=jax.ShapeDtypeStruct((m, d), BF16),
        grid=(m // tm, d // tn),
        in_specs=[pl.BlockSpec((tm, d), lambda i, j: (i, 0)),
                  pl.BlockSpec((tm, d), lambda i, j: (i, 0)),
                  pl.BlockSpec((d, tn), lambda i, j: (0, j), pipeline_mode=pl.Buffered(1)),
                  pl.BlockSpec((d, tn), lambda i, j: (0, j), pipeline_mode=pl.Buffered(1)),
                  pl.BlockSpec((tm, tn), lambda i, j: (i, gate_col * gpb + j)),
                  pl.BlockSpec((tm, tn), lambda i, j: (i, (gate_col + 1) * gpb + j))],
        out_specs=pl.BlockSpec((tm, tn), lambda i, j: (i, j)),
        compiler_params=_params("parallel", "arbitrary"),
        name="merge",
    )(ya, yb, wa, wb, proj, proj)


def kernel(x, c, ada_w, ada_b, norm1_g, ffn1_w1, ffn1_w3, ffn1_w2, norm2_g, w_in, dn_conv_w, dn_a_log, dn_dt_bias, dn_norm_g, mb_q_norm_g, mb_k_norm_g, rel_bias, w_proj_a, w_proj_b, w_out, norm3_g, ffn2_w1, ffn2_w3, ffn2_w2):
    b, t, d = x.shape
    m = b * t
    depth = ada_w.shape[0]
    wide = HEADS * HEAD_DIM
    cut0 = 4 * wide
    cut1 = cut0 + 2 * HEADS
    h = x.reshape(m, d)
    c_pad = jnp.zeros((8, d), F32).at[:b].set(c)
    for l in range(depth):
        ada = _ada(c_pad, ada_w, ada_b.reshape(depth, 1, -1), l)[:b]
        ada3 = ada.reshape(b, N_ADA, d)
        h = _ffn(h, ada3, norm1_g[l].reshape(1, d), ffn1_w1[l].astype(BF16), ffn1_w3[l].astype(BF16),
                 ffn1_w2[l].astype(BF16), slot=0, seq=t)

        w_in16 = w_in.astype(BF16)
        w_tail = w_in16[l][:, cut1:]
        w_ba = jnp.zeros((d, LANES), BF16).at[:, :2 * HEADS].set(w_in16[l][:, cut0:cut1])
        proj, ba = _inproj(h, ada3, norm2_g[l].reshape(1, d), w_in16, l, cut0, w_tail, w_ba, seq=t)
        proj3 = proj.reshape(b, t, -1)
        ba3 = ba.reshape(b, t, LANES)
        bat3 = jnp.transpose(ba3, (0, 2, 1))
        y_a = _deltanet(proj3, ba3, bat3, dn_conv_w[l], dn_a_log[l], dn_dt_bias[l], dn_norm_g[l])
        qn, kn, vt, kmean = _moba_prep(proj3, mb_q_norm_g[l], mb_k_norm_g[l])
        y_b = _moba(qn, kn, vt, kmean, rel_bias)
        merged = _merge(y_a.reshape(m, wide), y_b.reshape(m, wide), w_proj_a[l].astype(BF16),
                        w_proj_b[l].astype(BF16), proj, 7)
        h = _ffn(h, ada3, norm3_g[l].reshape(1, d), ffn2_w1[l].astype(BF16), ffn2_w3[l].astype(BF16),
                 ffn2_w2[l].astype(BF16), slot=6, seq=t, mix_x=merged, mix_w=w_out[l].astype(BF16))
    return h.reshape(b, t, d)
```

```python
import functools
import math

import jax
import jax.numpy as jnp
from jax import lax
from jax.experimental import pallas as pl
from jax.experimental.pallas import tpu as pltpu

F32 = jnp.float32
BF16 = jnp.bfloat16

LANES = 128
SUBLANES = 8
NORM_EPS = 1e-6
N_ADA = 9
HEADS = 16
HEAD_DIM = 128
DN_CONV = 4
DN_CHUNK = 128
DN_HEAD_BLOCK = 16
DN_INV_BASE = 8
MB_BLOCK = 256
MB_TOPK = 3
MB_HEAD_BLOCK = 8
MB_QBLOCKS = 1
MB_ISSUE_GROUP = 1
MB_ISSUE_AHEAD = 2
MB_ISSUE_AHEAD_NEAR = 1
MB_VROWS = HEAD_DIM + 16
RP_BUCKETS = 32
RP_MAX_DIST = 1024
NEG_BIG = -1e30
VMEM_LIMIT = 56 * 1024 * 1024

NT_DIMS = (((1,), (1,)), ((), ()))
NN_DIMS = (((1,), (0,)), ((), ()))
TN_DIMS = (((0,), (0,)), ((), ()))


def _dot(a, b, dims=NN_DIMS):
    return lax.dot_general(a.astype(BF16), b.astype(BF16), dims, preferred_element_type=F32)


def _split_bf16(x):
    hi = x.astype(BF16)
    lo = (x - hi.astype(F32)).astype(BF16)
    return hi, lo


def _dot3(a, b, dims=NN_DIMS):
    ah, al = _split_bf16(a)
    bh, bl = _split_bf16(b)
    f = lambda x, y: lax.dot_general(x, y, dims, preferred_element_type=F32)
    return f(ah, bh) + (f(ah, bl) + f(al, bh))


def _silu(x):
    return x * jax.nn.sigmoid(x)


def _params(*sem):
    return pltpu.CompilerParams(dimension_semantics=sem, vmem_limit_bytes=VMEM_LIMIT)


def _ada_kernel(c_ref, w_ref, b_ref, o_ref):
    s = _silu(c_ref[...])
    o_ref[...] = _dot3(s, w_ref[0]) + b_ref[0]


def _ada(c_pad, w, b, layer):
    rows, d = c_pad.shape
    n = w.shape[2]
    tn = 1024
    return pl.pallas_call(
        _ada_kernel,
        out_shape=jax.ShapeDtypeStruct((rows, n), F32),
        grid=(n // tn,),
        in_specs=[pl.BlockSpec((rows, d), lambda j: (0, 0)),
                  pl.BlockSpec((1, d, tn), lambda j: (layer, 0, j)),
                  pl.BlockSpec((1, 1, tn), lambda j: (layer, 0, j))],
        out_specs=pl.BlockSpec((rows, tn), lambda j: (0, j)),
        compiler_params=_params("arbitrary"),
        name="ada_proj",
    )(c_pad, w, b)


def _norm_modulate(h, gain, shift, scale):
    ms = jnp.mean(h * h, axis=-1, keepdims=True)
    return (h * lax.rsqrt(ms + NORM_EPS)) * (gain * (1.0 + scale)) + shift


def _ffn_kernel(*refs, slot, nf, mixer):
    if mixer:
        h_ref, x_ref, wm_ref, ada_ref, g_ref, w1_ref, w3_ref, w2_ref, o_ref, u_ref, acc_ref = refs
    else:
        h_ref, ada_ref, g_ref, w1_ref, w3_ref, w2_ref, o_ref, u_ref, acc_ref = refs
    f = pl.program_id(1)

    @pl.when(f == 0)
    def _():
        r = h_ref[...]
        if mixer:
            r = r + ada_ref[0, slot - 1:slot, :] * jnp.dot(x_ref[...], wm_ref[...], preferred_element_type=F32)
            o_ref[...] = r
        u = _norm_modulate(r, g_ref[...], ada_ref[0, slot:slot + 1, :], ada_ref[0, slot + 1:slot + 2, :])
        u_ref[...] = u.astype(BF16)
        acc_ref[...] = jnp.zeros_like(acc_ref)

    u = u_ref[...]
    a = jnp.dot(u, w1_ref[...], preferred_element_type=F32)
    b = jnp.dot(u, w3_ref[...], preferred_element_type=F32)
    acc_ref[...] += jnp.dot((_silu(a) * b).astype(BF16), w2_ref[...], preferred_element_type=F32)

    @pl.when(f == nf - 1)
    def _():
        r = o_ref[...] if mixer else h_ref[...]
        o_ref[...] = r + 0.5 * ada_ref[0, slot + 2:slot + 3, :] * acc_ref[...]


def _ffn(h, ada3, gain, w1, w3, w2, *, slot, seq, mix_x=None, mix_w=None):
    m, d = h.shape
    dff = w1.shape[1]
    tm, tf = 512, 512
    nf = dff // tf
    per_batch = seq // tm
    mixer = mix_x is not None
    row = lambda shape: pl.BlockSpec(shape, lambda i, f: (i, 0))
    mix_specs = [row((tm, mix_x.shape[1])),
                 pl.BlockSpec(mix_w.shape, lambda i, f: (0, 0), pipeline_mode=pl.Buffered(1))] if mixer else []
    return pl.pallas_call(
        functools.partial(_ffn_kernel, slot=slot, nf=nf, mixer=mixer),
        out_shape=jax.ShapeDtypeStruct((m, d), F32),
        grid=(m // tm, nf),
        in_specs=[row((tm, d))] + mix_specs + [
                  pl.BlockSpec((1, N_ADA, d), lambda i, f: (i // per_batch, 0, 0)),
                  pl.BlockSpec((1, d), lambda i, f: (0, 0)),
                  pl.BlockSpec((d, tf), lambda i, f: (0, f)),
                  pl.BlockSpec((d, tf), lambda i, f: (0, f)),
                  pl.BlockSpec((tf, d), lambda i, f: (f, 0))],
        out_specs=row((tm, d)),
        scratch_shapes=[pltpu.VMEM((tm, d), BF16), pltpu.VMEM((tm, d), F32)],
        compiler_params=_params("parallel", "arbitrary"),
        name="ffn",
    )(*([h] + ([mix_x, mix_w] if mixer else []) + [ada3, gain, w1, w3, w2]))


def _inproj_kernel(h_ref, ada_ref, g_ref, wa_ref, wb_ref, wba_ref, o_ref, ba_ref, u_ref, *, na):
    j = pl.program_id(1)

    @pl.when(j == 0)
    def _():
        u = _norm_modulate(h_ref[...], g_ref[...], ada_ref[0, 3:4, :], ada_ref[0, 4:5, :])
        u_ref[...] = u.astype(BF16)
        ba_ref[...] = jnp.dot(u_ref[...], wba_ref[...], preferred_element_type=F32)

    w = jnp.where(j < na, wa_ref[0], wb_ref[...])
    o_ref[...] = jnp.dot(u_ref[...], w, preferred_element_type=F32).astype(o_ref.dtype)


def _inproj(h, ada3, gain, w_all, layer, n_lead, wb, wba, *, seq):
    m, d = h.shape
    tm, tn = 1024, 1024
    na = n_lead // tn
    n = n_lead + wb.shape[1]
    per_batch = seq // tm
    return pl.pallas_call(
        functools.partial(_inproj_kernel, na=na),
        out_shape=(jax.ShapeDtypeStruct((m, n), BF16), jax.ShapeDtypeStruct((m, LANES), F32)),
        grid=(m // tm, n // tn),
        in_specs=[pl.BlockSpec((tm, d), lambda i, j: (i, 0)),
                  pl.BlockSpec((1, N_ADA, d), lambda i, j: (i // per_batch, 0, 0)),
                  pl.BlockSpec((1, d), lambda i, j: (0, 0)),
                  pl.BlockSpec((1, d, tn), lambda i, j: (layer, 0, jnp.minimum(j, na - 1))),
                  pl.BlockSpec((d, tn), lambda i, j: (0, jnp.maximum(j - na, 0))),
                  pl.BlockSpec((d, LANES), lambda i, j: (0, 0))],
        out_specs=(pl.BlockSpec((tm, tn), lambda i, j: (i, j)),
                   pl.BlockSpec((tm, LANES), lambda i, j: (i, 0))),
        scratch_shapes=[pltpu.VMEM((tm, d), BF16)],
        compiler_params=_params("parallel", "arbitrary"),
        name="in_proj",
    )(h, ada3, gain, w_all, wb, wba)


def _softplus(x):
    return jnp.maximum(x, 0.0) + jnp.log1p(jnp.exp(-jnp.abs(x)))


def _dn_kernel(q_ref, k_ref, v_ref, z_ref, ba_ref, bat_ref, wq_ref, wk_ref, wv_ref,
               arow_ref, dtrow_ref, acol_ref, dtcol_ref, gain_ref, o_ref,
               prev_ref, state_ref, gcrow_ref, *, hb):
    c = DN_CHUNK
    hg = pl.program_id(1)
    n = pl.program_id(2)

    @pl.when(n == 0)
    def _():
        prev_ref[...] = jnp.zeros_like(prev_ref)
        state_ref[...] = jnp.zeros_like(state_ref)

    width = hb * HEAD_DIM
    sub = prev_ref.shape[1]
    row_t = lax.broadcasted_iota(jnp.int32, (sub, width), 0)

    def conv_silu(x_ref, w_ref, s):
        cur = x_ref[0].astype(F32)
        tail = prev_ref[s]
        acc = cur * w_ref[DN_CONV - 1:DN_CONV, :]
        for d in range(1, DN_CONV):
            shifted = pltpu.roll(cur, d, 0)
            head = jnp.where(row_t < d, pltpu.roll(tail, d, 0), shifted[:sub])
            shifted = jnp.concatenate([head, shifted[sub:]], axis=0)
            acc = acc + shifted * w_ref[DN_CONV - 1 - d:DN_CONV - d, :]
        prev_ref[s] = cur[c - sub:]
        return _silu(acc)

    xq = conv_silu(q_ref, wq_ref, 0)
    xk = conv_silu(k_ref, wk_ref, 1)
    xv = conv_silu(v_ref, wv_ref, 2)

    slab = ba_ref[0]
    beta_mat = jax.nn.sigmoid(slab)
    g_mat = -jnp.exp(arow_ref[...]) * _softplus(slab + dtrow_ref[...])
    ri = lax.broadcasted_iota(jnp.int32, (c, c), 0)
    ci = lax.broadcasted_iota(jnp.int32, (c, c), 1)
    lower_incl = (ri >= ci)
    tri_l = jnp.where(lower_incl, 1.0, 0.0).astype(BF16)
    tri_u = jnp.where(ri <= ci, 1.0, 0.0).astype(BF16)

    def exact_ones_dot(x, ones, left):
        x1 = x.astype(BF16)
        r1 = x - x1.astype(F32)
        x2 = r1.astype(BF16)
        x3 = (r1 - x2.astype(F32)).astype(BF16)
        if left:
            f = lambda p: jnp.dot(ones, p, preferred_element_type=F32)
        else:
            f = lambda p: jnp.dot(p, ones, preferred_element_type=F32)
        return f(x1) + (f(x2) + f(x3))

    gc_colmat = exact_ones_dot(g_mat, tri_l, True)
    a_rows = bat_ref[0, HEADS:2 * HEADS, :]
    g_rows = -jnp.exp(acol_ref[...]) * _softplus(a_rows + dtcol_ref[...])
    gcrow_ref[...] = exact_ones_dot(g_rows, tri_u, False)

    lane = lax.broadcasted_iota(jnp.int32, (c, LANES), 1)
    eye = jnp.where(ri == ci, 1.0, 0.0)
    strict = ri > ci
    scale = HEAD_DIM ** -0.5

    hs = []
    for hh in range(hb):
        h = hg * hb + hh
        sl = slice(hh * HEAD_DIM, (hh + 1) * HEAD_DIM)
        beta = jnp.sum(jnp.where(lane == h, beta_mat, 0.0), axis=1, keepdims=True)
        gcc = jnp.sum(jnp.where(lane == HEADS + h, gc_colmat, 0.0), axis=1, keepdims=True)
        gcr = gcrow_ref[pl.ds(h, 1), :]
        gc_last = gcr[:, c - 1:c]
        decay = jnp.where(lower_incl, jnp.exp(jnp.minimum(gcc - gcr, 0.0)), 0.0)
        q = xq[:, sl]
        k = xk[:, sl]
        q = q * (lax.rsqrt(jnp.sum(q * q, axis=-1, keepdims=True) + NORM_EPS) * scale)
        k = k * lax.rsqrt(jnp.sum(k * k, axis=-1, keepdims=True) + NORM_EPS)
        kb = k * beta
        egc = jnp.exp(gcc)
        hs.append(dict(hh=hh, sl=sl, decay=decay, q=q, k=k, kb=kb,
                       rhs=jnp.concatenate([xv[:, sl] * beta, kb * egc], axis=1),
                       qd=q * egc, kd=k * jnp.exp(gc_last - gcc), g_tot=jnp.exp(gc_last)))

    for s in hs:
        s["lmat"] = jnp.where(strict, _dot(s["kb"], s["k"], NT_DIMS) * s["decay"], 0.0)
        s["qk"] = _dot(s["q"], s["k"], NT_DIMS) * s["decay"]
    shift0 = int(math.log2(DN_INV_BASE))
    mm = lambda a, b: jnp.dot(a, b, preferred_element_type=F32)
    same_block = jnp.where((ri >> shift0) == (ci >> shift0), 1.0, 0.0).astype(BF16)
    for s in hs:
        s["lb"] = s["lmat"].astype(BF16)
        s["pw"] = -(s["lb"] * same_block)
        s["tmat"] = eye + s["pw"].astype(F32)
    for _ in range(shift0 - 1):
        for s in hs:
            s["pw"] = mm(s["pw"], s["pw"]).astype(BF16)
        for s in hs:
            s["tmat"] = s["tmat"] + mm(s["tmat"].astype(BF16), s["pw"])
    for sh in range(shift0, int(math.log2(c))):
        lower_left = jnp.where(jnp.logical_and((ri >> sh) == (ci >> sh) + 1, ((ci >> sh) & 1) == 0),
                               1.0, 0.0).astype(BF16)
        for s in hs:
            s["tb"] = s["tmat"].astype(BF16)
            s["pw"] = mm(s["lb"] * lower_left, s["tb"]).astype(BF16)
        for s in hs:
            s["tmat"] = s["tmat"] - mm(s["tb"], s["pw"])
    for s in hs:
        s["uw"] = _dot(s["tmat"], s["rhs"])
    for s in hs:
        s["state"] = state_ref[s["hh"]]
        s["v_new"] = s["uw"][:, :HEAD_DIM] - _dot(s["uw"][:, HEAD_DIM:], s["state"])
    for s in hs:
        s["o"] = _dot(s["qd"], s["state"]) + _dot(s["qk"], s["v_new"])
        state_ref[s["hh"]] = s["state"] * s["g_tot"] + _dot(s["kd"], s["v_new"], TN_DIMS)
    for s in hs:
        o = s["o"]
        o = o * lax.rsqrt(jnp.mean(o * o, axis=-1, keepdims=True) + NORM_EPS) * gain_ref[...]
        o_ref[0, :, s["sl"]] = (o * _silu(z_ref[0, :, s["sl"]].astype(F32))).astype(o_ref.dtype)


def _deltanet(proj3, ba3, bat3, conv_w, a_log, dt_bias, gain):
    b, t, _ = proj3.shape
    hb = DN_HEAD_BLOCK
    c = DN_CHUNK
    width = hb * HEAD_DIM
    ng = HEADS // hb
    region = (HEADS * HEAD_DIM) // width

    def pad_row(x):
        return jnp.zeros((1, LANES), F32).at[0, HEADS:2 * HEADS].set(x.astype(F32))

    arow, dtrow = pad_row(a_log), pad_row(dt_bias)
    acol = a_log.astype(F32).reshape(HEADS, 1)
    dtcol = dt_bias.astype(F32).reshape(HEADS, 1)
    gain2 = gain.astype(F32).reshape(1, HEAD_DIM)

    def col(r):
        return pl.BlockSpec((1, c, width), lambda bi, g, n, r=r: (bi, n, r * region + g))

    def wcol(r):
        return pl.BlockSpec((DN_CONV, width), lambda bi, g, n, r=r: (0, r * region + g))

    const = lambda shape: pl.BlockSpec(shape, lambda bi, g, n: tuple(0 for _ in shape))
    return pl.pallas_call(
        functools.partial(_dn_kernel, hb=hb),
        out_shape=jax.ShapeDtypeStruct((b, t, HEADS * HEAD_DIM), BF16),
        grid=(b, ng, t // c),
        in_specs=[col(0), col(1), col(2), col(3),
                  pl.BlockSpec((1, c, LANES), lambda bi, g, n: (bi, n, 0)),
                  pl.BlockSpec((1, LANES, c), lambda bi, g, n: (bi, 0, n)),
                  wcol(0), wcol(1), wcol(2),
                  const((1, LANES)), const((1, LANES)), const((HEADS, 1)), const((HEADS, 1)),
                  const((1, HEAD_DIM))],
        out_specs=pl.BlockSpec((1, c, width), lambda bi, g, n: (bi, n, g)),
        scratch_shapes=[pltpu.VMEM((3, SUBLANES, width), F32),
                        pltpu.VMEM((hb, HEAD_DIM, HEAD_DIM), F32),
                        pltpu.VMEM((HEADS, c), F32)],
        compiler_params=_params("parallel", "parallel", "arbitrary"),
        name="deltanet",
    )(proj3, proj3, proj3, proj3, ba3, bat3, conv_w, conv_w, conv_w, arow, dtrow, acol, dtcol, gain2)


def _moba_prep_kernel(q_ref, k_ref, v_ref, qg_ref, kg_ref, qn_ref, kn_ref, vt_ref, km_ref):
    n = pl.program_id(1)
    scale = HEAD_DIM ** -0.5

    @pl.when(n == 0)
    def _():
        km_ref[...] = jnp.zeros_like(km_ref)

    row = lax.broadcasted_iota(jnp.int32, (km_ref.shape[1], HEAD_DIM), 0)
    ones = jnp.ones((MB_VROWS - HEAD_DIM, MB_BLOCK), BF16)
    lane = lax.broadcasted_iota(jnp.int32, (MB_BLOCK, LANES), 1)
    onehot = jnp.where(lane == n, 1.0, 0.0).astype(BF16)
    for h in range(HEADS):
        sl = slice(h * HEAD_DIM, (h + 1) * HEAD_DIM)
        ka = 2 * h * HEAD_DIM
        q = q_ref[0, :, sl].astype(F32)
        k = k_ref[0, :, sl].astype(F32)
        qn = q * lax.rsqrt(jnp.mean(q * q, axis=-1, keepdims=True) + NORM_EPS) * qg_ref[...]
        kn = k * lax.rsqrt(jnp.mean(k * k, axis=-1, keepdims=True) + NORM_EPS) * kg_ref[...]
        qn_ref[0, :, sl] = (qn * scale).astype(qn_ref.dtype)
        kn_ref[0, 0, :, ka:ka + HEAD_DIM] = kn.astype(BF16)
        kn_ref[0, 0, :, ka + HEAD_DIM:ka + 2 * HEAD_DIM] = onehot
        km_ref[0, :, sl] = jnp.where(row == n, jnp.mean(kn, axis=0, keepdims=True), km_ref[0, :, sl])
        vt_ref[0, 0, h, :HEAD_DIM, :] = v_ref[0, :, sl].astype(F32).T.astype(BF16)
        vt_ref[0, 0, h, HEAD_DIM:, :] = ones


def _moba_prep(proj3, q_gain, k_gain):
    b, t, _ = proj3.shape
    nb = t // MB_BLOCK
    w = HEADS * HEAD_DIM
    col = lambda r: pl.BlockSpec((1, MB_BLOCK, w), lambda bi, n, r=r: (bi, n, r))
    gspec = pl.BlockSpec((1, HEAD_DIM), lambda bi, n: (0, 0))
    return pl.pallas_call(
        _moba_prep_kernel,
        out_shape=(jax.ShapeDtypeStruct((b, t, w), BF16),
                   jax.ShapeDtypeStruct((b, nb, MB_BLOCK, 2 * w), BF16),
                   jax.ShapeDtypeStruct((b, nb, HEADS, MB_VROWS, MB_BLOCK), BF16),
                   jax.ShapeDtypeStruct((b, nb, w), F32)),
        grid=(b, nb),
        in_specs=[col(4), col(5), col(6), gspec, gspec],
        out_specs=(pl.BlockSpec((1, MB_BLOCK, w), lambda bi, n: (bi, n, 0)),
                   pl.BlockSpec((1, 1, MB_BLOCK, 2 * w), lambda bi, n: (bi, n, 0, 0)),
                   pl.BlockSpec((1, 1, HEADS, MB_VROWS, MB_BLOCK), lambda bi, n: (bi, n, 0, 0, 0)),
                   pl.BlockSpec((1, nb, w), lambda bi, n: (bi, 0, 0))),
        compiler_params=_params("parallel", "arbitrary"),
        name="moba_prep",
    )(proj3, proj3, proj3, q_gain.astype(F32).reshape(1, HEAD_DIM), k_gain.astype(F32).reshape(1, HEAD_DIM))


def _bucket_thresholds():
    max_exact = RP_BUCKETS // 2
    th = list(range(1, max_exact + 1))
    for kk in range(1, RP_BUCKETS - max_exact):
        x = max_exact * (RP_MAX_DIST / max_exact) ** (kk / (RP_BUCKETS - max_exact))
        r = round(x)
        th.append(r if abs(x - r) < 1e-9 else math.ceil(x))
    return th


N_NEAR = 5
LOG2E = 1.4426950408889634


def _moba_kernel(rb_ref, q_ref, k_ref, vt_ref, km_ref, o_ref, tab_ref, qa_ref, m_ref, alpha_ref, acc_ref, p_ref,
                 *, nb, hb):
    hg = pl.program_id(0)
    bi = pl.program_id(1)
    it = pl.program_id(2)
    blk = MB_BLOCK
    nq = MB_QBLOCKS
    tq = nq * blk
    shift = int(math.log2(blk))
    last = nq * it + nq - 1
    thresholds = _bucket_thresholds()
    heads = list(range(hb))
    hsl = lambda hh: slice(hh * HEAD_DIM, (hh + 1) * HEAD_DIM)

    @pl.when(jnp.logical_and(bi == 0, it == 0))
    def _():
        ko = lax.broadcasted_iota(jnp.int32, (blk, blk), 0)
        qo = lax.broadcasted_iota(jnp.int32, (blk, blk), 1)
        for hh in heads:
            h = hg * hb + hh
            far = rb_ref[RP_BUCKETS - 1, h]
            for delta in range(N_NEAR):
                d = delta * blk + qo - ko
                acc = jnp.full((blk, blk), rb_ref[0, h], F32)
                for bkt in range(1, RP_BUCKETS):
                    if delta * blk + blk - 1 >= thresholds[bkt - 1]:
                        acc = jnp.where(d >= thresholds[bkt - 1], rb_ref[bkt, h], acc)
                acc = (acc - far) * LOG2E
                if delta == 0:
                    acc = jnp.where(d >= 0, acc, NEG_BIG)
                tab_ref[hh, delta] = acc
            tab_ref[hh, N_NEAR] = jnp.zeros((blk, blk), F32)

    bidx = lax.broadcasted_iota(jnp.int32, (nb, tq), 0)
    qblk = nq * it + (lax.broadcasted_iota(jnp.int32, (nb, tq), 1) >> shift)
    past = bidx < qblk
    gates = []
    for hh in heads:
        kmh, kml = _split_bf16(km_ref[0, :, hsl(hh)])
        qb = q_ref[0, :, hsl(hh)]
        gate = (lax.dot_general(kmh, qb, NT_DIMS, preferred_element_type=F32)
                + lax.dot_general(kml, qb, NT_DIMS, preferred_element_type=F32))
        gates.append(jnp.where(past, gate, -jnp.inf))
    for hh, gate in zip(heads, gates):
        rank = jnp.zeros((nb, tq), jnp.int32)
        for jp in range(nb):
            gj = gate[jp:jp + 1, :]
            ahead = jnp.logical_or(gj > gate, jnp.logical_and(gj == gate, jp < bidx))
            rank = rank + ahead.astype(jnp.int32)
        allowed = jnp.logical_or(jnp.logical_and(past, rank < MB_TOPK), bidx == qblk)
        mask = jnp.where(allowed, 0.0, NEG_BIG)
        q_t = (q_ref[0, :, hsl(hh)].astype(F32) * LOG2E).T.astype(BF16)
        mask_rows = jnp.concatenate([mask, jnp.zeros((LANES - nb, tq), F32)], axis=0).astype(BF16)
        qa_ref[hh] = jnp.concatenate([q_t, mask_rows], axis=0)

    m_ref[...] = jnp.full(m_ref.shape, NEG_BIG, F32)
    acc_ref[...] = jnp.zeros(acc_ref.shape, F32)

    def logits(t, near, hs):
        j = last - t
        ss = [jnp.dot(k_ref[0, j, :, 2 * hh * HEAD_DIM:2 * (hh + 1) * HEAD_DIM], qa_ref[hh],
                      preferred_element_type=F32) for hh in hs]
        if near:
            idx = [jnp.clip(t - (nq - 1 - a), 0, N_NEAR) for a in range(nq)]
            ss = [jnp.concatenate([s[:, a * blk:(a + 1) * blk] + tab_ref[hh, idx[a]] for a in range(nq)], axis=1)
                  for hh, s in zip(hs, ss)]
        return ss

    def values(t, hs):
        slot = t & 1
        return [jnp.dot(vt_ref[0, last - t, hh], p_ref[slot, hh], preferred_element_type=F32) for hh in hs]

    def probabilities(t, ss, hs):
        slot = t & 1
        alphas = []
        for hh, s in zip(hs, ss):
            for q0 in range(0, tq, LANES):
                qs = slice(q0, q0 + LANES)
                m_old = m_ref[hh, :, qs]
                m_new = jnp.maximum(m_old, jnp.max(s[:, qs], axis=0, keepdims=True))
                m_ref[hh, :, qs] = m_new
                p_ref[slot, hh, :, qs] = jnp.exp2(s[:, qs] - m_new).astype(BF16)
                alphas.append((hh, qs, jnp.exp2(m_old - m_new)))
        return alphas

    def accumulate(pvs, hs):
        for hh, pv in zip(hs, pvs):
            acc_ref[hh] = alpha_ref[hh] * acc_ref[hh] + pv

    def set_alpha(alphas):
        for hh, qs, alpha in alphas:
            alpha_ref[hh, :, qs] = alpha

    groups = [heads[g:g + MB_ISSUE_GROUP] for g in range(0, hb, MB_ISSUE_GROUP)]

    def make_body(near):
        n_ahead = MB_ISSUE_AHEAD_NEAR if near else MB_ISSUE_AHEAD

        def body(t, carry):
            ahead = [logits(t, near, grp) for grp in groups[:n_ahead]]
            for gi, grp in enumerate(groups):
                if gi + n_ahead < len(groups):
                    ahead.append(logits(t, near, groups[gi + n_ahead]))
                pvs = values(t - 1, grp)
                alphas = probabilities(t, ahead[gi], grp)
                accumulate(pvs, grp)
                set_alpha(alphas)
            return carry
        return body

    set_alpha(probabilities(0, logits(0, True, heads), heads))
    n_near = jnp.minimum(last + 1, N_NEAR + nq - 1)
    lax.fori_loop(1, n_near, make_body(True), 0)
    lax.fori_loop(n_near, last + 1, make_body(False), 0)
    accumulate(values(last, heads), heads)
    for hh in heads:
        acc = acc_ref[hh]
        o_ref[0, :, hsl(hh)] = (acc[:HEAD_DIM] / acc[HEAD_DIM:HEAD_DIM + 1]).T.astype(o_ref.dtype)


def _moba(qn, kn, vt, kmean, rel_bias):
    b, t, w = qn.shape
    nb = t // MB_BLOCK
    hb = MB_HEAD_BLOCK
    hw = hb * HEAD_DIM
    tq = MB_QBLOCKS * MB_BLOCK
    return pl.pallas_call(
        functools.partial(_moba_kernel, nb=nb, hb=hb),
        out_shape=jax.ShapeDtypeStruct((b, t, w), BF16),
        grid=(HEADS // hb, b, t // tq),
        in_specs=[pl.BlockSpec(memory_space=pltpu.SMEM),
                  pl.BlockSpec((1, tq, hw), lambda g, bi, i: (bi, i, g)),
                  pl.BlockSpec((1, nb, MB_BLOCK, 2 * hw), lambda g, bi, i: (bi, 0, 0, g),
                               pipeline_mode=pl.Buffered(1)),
                  pl.BlockSpec((1, nb, hb, MB_VROWS, MB_BLOCK), lambda g, bi, i: (bi, 0, g, 0, 0),
                               pipeline_mode=pl.Buffered(1)),
                  pl.BlockSpec((1, nb, hw), lambda g, bi, i: (bi, 0, g))],
        out_specs=pl.BlockSpec((1, tq, hw), lambda g, bi, i: (bi, i, g)),
        scratch_shapes=[pltpu.VMEM((hb, N_NEAR + 1, MB_BLOCK, MB_BLOCK), F32),
                        pltpu.VMEM((hb, 2 * HEAD_DIM, tq), BF16),
                        pltpu.VMEM((hb, 1, tq), F32),
                        pltpu.VMEM((hb, 1, tq), F32),
                        pltpu.VMEM((hb, MB_VROWS, tq), F32),
                        pltpu.VMEM((2, hb, MB_BLOCK, tq), BF16)],
        compiler_params=_params("arbitrary", "arbitrary", "arbitrary"),
        name="moba_attn",
    )(rel_bias.astype(F32), qn, kn, vt, kmean)


def _merge_kernel(ya_ref, yb_ref, wa_ref, wb_ref, ga_ref, gb_ref, o_ref):
    pa = jnp.dot(ya_ref[...], wa_ref[...], preferred_element_type=F32)
    pb = jnp.dot(yb_ref[...], wb_ref[...], preferred_element_type=F32)
    ga = ga_ref[...].astype(F32)
    gb = gb_ref[...].astype(F32)
    o_ref[...] = (jax.nn.sigmoid(ga) * pa + jax.nn.sigmoid(gb) * pb).astype(o_ref.dtype)


def _merge(ya, yb, wa, wb, proj, gate_col):
    m, d = ya.shape
    tm, tn = 512, d
    gpb = d // tn
    return pl.pallas_call(
        _merge_kernel,
        out_shape=jax.ShapeDtypeStruct((m, d), BF16),
        grid=(m // tm, d // tn),
        in_specs=[pl.BlockSpec((tm, d), lambda i, j: (i, 0)),
                  pl.BlockSpec((tm, d), lambda i, j: (i, 0)),
                  pl.BlockSpec((d, tn), lambda i, j: (0, j), pipeline_mode=pl.Buffered(1)),
                  pl.BlockSpec((d, tn), lambda i, j: (0, j), pipeline_mode=pl.Buffered(1)),
                  pl.BlockSpec((tm, tn), lambda i, j: (i, gate_col * gpb + j)),
                  pl.BlockSpec((tm, tn), lambda i, j: (i, (gate_col + 1) * gpb + j))],
        out_specs=pl.BlockSpec((tm, tn), lambda i, j: (i, j)),
        compiler_params=_params("parallel", "arbitrary"),
        name="merge",
    )(ya, yb, wa, wb, proj, proj)


def kernel(x, c, ada_w, ada_b, norm1_g, ffn1_w1, ffn1_w3, ffn1_w2, norm2_g, w_in, dn_conv_w, dn_a_log, dn_dt_bias, dn_norm_g, mb_q_norm_g, mb_k_norm_g, rel_bias, w_proj_a, w_proj_b, w_out, norm3_g, ffn2_w1, ffn2_w3, ffn2_w2):
    b, t, d = x.shape
    m = b * t
    depth = ada_w.shape[0]
    wide = HEADS * HEAD_DIM
    cut0 = 4 * wide
    cut1 = cut0 + 2 * HEADS
    h = x.reshape(m, d)
    c_pad = jnp.zeros((8, d), F32).at[:b].set(c)
    for l in range(depth):
        ada = _ada(c_pad, ada_w, ada_b.reshape(depth, 1, -1), l)[:b]
        ada3 = ada.reshape(b, N_ADA, d)
        h = _ffn(h, ada3, norm1_g[l].reshape(1, d), ffn1_w1[l].astype(BF16), ffn1_w3[l].astype(BF16),
                 ffn1_w2[l].astype(BF16), slot=0, seq=t)

        w_in16 = w_in.astype(BF16)
        w_tail = w_in16[l][:, cut1:]
        w_ba = jnp.zeros((d, LANES), BF16).at[:, :2 * HEADS].set(w_in16[l][:, cut0:cut1])
        proj, ba = _inproj(h, ada3, norm2_g[l].reshape(1, d), w_in16, l, cut0, w_tail, w_ba, seq=t)
        proj3 = proj.reshape(b, t, -1)
        ba3 = ba.reshape(b, t, LANES)
        bat3 = jnp.transpose(ba3, (0, 2, 1))
        y_a = _deltanet(proj3, ba3, bat3, dn_conv_w[l], dn_a_log[l], dn_dt_bias[l], dn_norm_g[l])
        qn, kn, vt, kmean = _moba_prep(proj3, mb_q_norm_g[l], mb_k_norm_g[l])
        y_b = _moba(qn, kn, vt, kmean, rel_bias)
        merged = _merge(y_a.reshape(m, wide), y_b.reshape(m, wide), w_proj_a[l].astype(BF16),
                        w_proj_b[l].astype(BF16), proj, 7)
        h = _ffn(h, ada3, norm3_g[l].reshape(1, d), ffn2_w1[l].astype(BF16), ffn2_w3[l].astype(BF16),
                 ffn2_w2[l].astype(BF16), slot=6, seq=t, mix_x=merged, mix_w=w_out[l].astype(BF16))
    return h.reshape(b, t, d)
```
